```python
import functools
import jax, jax.numpy as jnp
from jax import lax
import numpy as np

D_MODEL = 1024
BATCH = 4
SEQ = 4096
DEPTH = 4
DEC_BATCH = 32
DEC_SEQ = 1
PAST_LEN = 8192
PAGE_SIZE = 128

HEAD_DIM = 64
ROT_DIM = HEAD_DIM // 4
ROPE_THETA = 500000.0
A_HEADS = 8
A_KV = 2
IDX_HEADS = 8
IDX_DIM = 64
DSA_TOPK = 256
B_HEADS = 8
B_KV = 2
CMP_STRIDE = 16
CMP_LEN = 2 * CMP_STRIDE
SLC_BLK = 64
SLC_TOPN = 16
WINDOW = 512
D_FF = 2816
Q_BLOCK = 128
EPS = 1e-6
NEG = -1e30
A_WIDTH = A_HEADS * HEAD_DIM
B_WIDTH = B_HEADS * HEAD_DIM
PROJ_SPLITS = (A_WIDTH, A_KV * HEAD_DIM, A_KV * HEAD_DIM, IDX_HEADS * IDX_DIM, IDX_DIM, IDX_HEADS,
               B_WIDTH, B_KV * HEAD_DIM, B_KV * HEAD_DIM, B_KV * HEAD_DIM, B_KV * HEAD_DIM,
               B_KV * HEAD_DIM, B_KV * HEAD_DIM, B_HEADS * 3, 2 * D_MODEL)
D_IN = sum(PROJ_SPLITS)

kernel_name = "hybrid_dsa_nsa_macaron_step"


def rms_norm(x, g):
    x32 = x.astype(jnp.float32)
    y = x32 * lax.rsqrt(jnp.mean(x32 * x32, axis=-1, keepdims=True) + EPS)
    return (y * g.astype(jnp.float32)).astype(x.dtype)


def swiglu(x, w_up, w_down):
    a, b = jnp.split(x @ w_up, 2, axis=-1)
    return (jax.nn.silu(a) * b) @ w_down


def rope(x, pos):
    half = ROT_DIM // 2
    inv = ROPE_THETA ** (-jnp.arange(half, dtype=jnp.float32) / half)
    ang = pos.astype(jnp.float32)[:, None] * inv
    cos, sin = jnp.cos(ang)[:, None, :], jnp.sin(ang)[:, None, :]
    x32 = x.astype(jnp.float32)
    x1, x2, rest = x32[..., :half], x32[..., half:ROT_DIM], x32[..., ROT_DIM:]
    return jnp.concatenate([x1 * cos - x2 * sin, x1 * sin + x2 * cos, rest], axis=-1).astype(x.dtype)


def masked_softmax(s, mask):
    p = jax.nn.softmax(jnp.where(mask, s, NEG), axis=-1)
    return jnp.where(mask, p, 0.0)


def dense_rows(rows, pos, head=None):
    b = jnp.arange(rows.shape[0]).reshape((-1,) + (1,) * (pos.ndim - 1))
    pc = jnp.clip(pos, 0, rows.shape[1] - 1)
    return rows[b, pc] if head is None else rows[b, pc, head]


def paged_rows(pool, page_table, new_rows, pos, head=None):
    page = pool.shape[1]
    past = page_table.shape[1] * page
    b = jnp.arange(pos.shape[0]).reshape((-1,) + (1,) * (pos.ndim - 1))
    pc = jnp.clip(pos, 0, past - 1)
    phys = page_table[b, pc // page]
    nc = jnp.clip(pos - past, 0, new_rows.shape[1] - 1)
    if head is None:
        old, new = pool[phys, pc % page], new_rows[b, nc]
    else:
        old, new = pool[phys, pc % page, head], new_rows[b, nc, head]
    keep = (pos < past).reshape(pos.shape + (1,) * (old.ndim - pos.ndim))
    return jnp.where(keep, old, new)


def split_proj(h, w_in, pos):
    N, T = h.shape[:2]
    offs = [int(o) for o in np.cumsum(PROJ_SPLITS)[:-1]]
    (qa, ka, va, iq, ik, iw, qb, ck, cv, sk, sv, wk, wv, gb, gate) = jnp.split(h @ w_in, offs, axis=-1)
    hd = lambda t, H: t.reshape(N, T, H, -1)
    qb = hd(qb, B_HEADS)
    parts = (rope(hd(qa, A_HEADS), pos), rope(hd(ka, A_KV), pos), hd(va, A_KV),
             rope(hd(iq, IDX_HEADS), pos), rope(hd(ik, 1), pos)[:, :, 0], iw,
             qb, rope(qb, pos), hd(ck, B_KV), hd(cv, B_KV),
             rope(hd(sk, B_KV), pos), hd(sv, B_KV), rope(hd(wk, B_KV), pos), hd(wv, B_KV),
             hd(gb, B_HEADS))
    return parts, gate


def dsa_core(q, iq, iw, pos_q, ik, k_sel, gather):
    N, Tq = q.shape[:2]
    L = ik.shape[1]
    G = A_HEADS // A_KV
    dots = jnp.einsum('nthd,nld->nthl', iq, ik, preferred_element_type=jnp.float32) * IDX_DIM ** -0.5
    score = jnp.einsum('nth,nthl->ntl', iw.astype(jnp.float32) * IDX_HEADS ** -0.5, jax.nn.relu(dots))
    vis = jnp.arange(L)[None, :] <= pos_q[:, None]
    score = jnp.where(vis[None], score, -jnp.inf)
    _, idx = lax.top_k(score, k_sel)
    valid = idx <= pos_q[None, :, None]
    kg, vg = gather(idx)
    qg = q.reshape(N, Tq, A_KV, G, HEAD_DIM)
    s = jnp.einsum('ntgrd,ntkgd->ntgrk', qg, kg, preferred_element_type=jnp.float32) * HEAD_DIM ** -0.5
    p = masked_softmax(s, valid[:, :, None, None, :])
    o = jnp.einsum('ntgrk,ntkgd->ntgrd', p.astype(vg.dtype), vg)
    return o.reshape(N, Tq, A_WIDTH)


def compress(rows, w):
    N, L = rows.shape[:2]
    nb = (L - CMP_LEN) // CMP_STRIDE + 1
    chunks = rows[:, :(nb + 1) * CMP_STRIDE].reshape(N, nb + 1, CMP_STRIDE, B_KV, HEAD_DIM)
    lo = jnp.einsum('ncigd,igd->ncgd', chunks, w[:CMP_STRIDE])
    hi = jnp.einsum('ncigd,igd->ncgd', chunks, w[CMP_STRIDE:])
    return lo[:, :nb] + hi[:, 1:]


def slc_overlap(n_cmp, n_keys):
    n_slc = -(-n_keys // SLC_BLK)
    j = np.arange(n_cmp)[:, None]
    s = np.arange(n_slc)[None, :]
    lo = np.maximum(j * CMP_STRIDE, s * SLC_BLK)
    hi = np.minimum(j * CMP_STRIDE + CMP_LEN, (s + 1) * SLC_BLK)
    return jnp.asarray(np.maximum(hi - lo, 0) / CMP_STRIDE, dtype=jnp.float32)


def nsa_core(q, qr, gb, pos_q, kc, vc, overlap, slc_gather, kw, vw, pos_w):
    N, Tq = q.shape[:2]
    G = B_HEADS // B_KV
    scale = HEAD_DIM ** -0.5
    qg = q.reshape(N, Tq, B_KV, G, HEAD_DIM)
    qrg = qr.reshape(N, Tq, B_KV, G, HEAD_DIM)
    nb = kc.shape[1]
    cmp_end = jnp.arange(nb) * CMP_STRIDE + CMP_LEN - 1
    vis_c = cmp_end[None, :] <= pos_q[:, None]
    s = jnp.einsum('ntgrd,ncgd->ntgrc', qg, kc, preferred_element_type=jnp.float32) * scale
    p_c = masked_softmax(s, vis_c[None, :, None, None, :])
    o_c = jnp.einsum('ntgrc,ncgd->ntgrd', p_c.astype(vc.dtype), vc)
    n_slc = overlap.shape[1]
    p_s = jnp.einsum('ntgc,cs->ntgs', p_c.sum(axis=3), overlap)
    blk = jnp.arange(n_slc)[None, :]
    cur = (pos_q // SLC_BLK)[:, None]
    forced = (blk == 0) | (blk == cur) | (blk == cur - 1)
    vis_b = blk * SLC_BLK <= pos_q[:, None]
    p_s = jnp.where(forced[None, :, None, :], 1e6, p_s)
    p_s = jnp.where(vis_b[None, :, None, :], p_s, NEG)
    n_sel = min(SLC_TOPN, n_slc)
    _, bidx = lax.top_k(p_s, n_sel)
    pos_s = (bidx[..., None] * SLC_BLK + jnp.arange(SLC_BLK)).reshape(N, Tq, B_KV, n_sel * SLC_BLK)
    ks, vs = slc_gather(pos_s)
    valid_s = pos_s <= pos_q[None, :, None, None]
    s = jnp.einsum('ntgrd,ntgkd->ntgrk', qrg, ks, preferred_element_type=jnp.float32) * scale
    p = masked_softmax(s, valid_s[:, :, :, None, :])
    o_s = jnp.einsum('ntgrk,ntgkd->ntgrd', p.astype(vs.dtype), vs)
    dist = pos_q[:, None] - pos_w[None, :]
    vis_w = (pos_w[None, :] >= 0) & (dist >= 0) & (dist <= WINDOW)
    s = jnp.einsum('ntgrd,nwgd->ntgrw', qrg, kw, preferred_element_type=jnp.float32) * scale
    p = masked_softmax(s, vis_w[None, :, None, None, :])
    o_w = jnp.einsum('ntgrw,nwgd->ntgrd', p.astype(vw.dtype), vw)
    g = jax.nn.sigmoid(gb.astype(jnp.float32)).reshape(N, Tq, B_KV, G, 3).astype(q.dtype)
    o = g[..., 0:1] * o_c + g[..., 1:2] * o_s + g[..., 2:3] * o_w
    return o.reshape(N, Tq, B_WIDTH)


def unblock(o):
    nblk, N, qb, W = o.shape
    return o.transpose(1, 0, 2, 3).reshape(N, nblk * qb, W)


def mix_prompt(parts, w_ck, w_cv):
    qa, ka, va, iq, ik, iw, qb, qbr, ck, cv, sk, sv, wk, wv, gb = parts
    N, S = qa.shape[:2]
    n_blocks = S // Q_BLOCK
    k_sel = min(DSA_TOPK, S // 4)
    sl = lambda t, s0: lax.dynamic_slice_in_dim(t, s0, Q_BLOCK, axis=1)
    gather_a = lambda idx: (dense_rows(ka, idx), dense_rows(va, idx))

    def dsa_block(i):
        s0 = i * Q_BLOCK
        pos_q = s0 + jnp.arange(Q_BLOCK)
        return dsa_core(sl(qa, s0), sl(iq, s0), sl(iw, s0), pos_q, ik, k_sel, gather_a)

    o_a = unblock(lax.map(dsa_block, jnp.arange(n_blocks)))
    kc, vc = compress(ck, w_ck), compress(cv, w_cv)
    overlap = slc_overlap(kc.shape[1], S)
    pad = ((0, 0), (WINDOW, 0), (0, 0), (0, 0))
    kwp, vwp = jnp.pad(wk, pad), jnp.pad(wv, pad)
    head = jnp.arange(B_KV).reshape(1, 1, B_KV, 1)
    gather_b = lambda p: (dense_rows(sk, p, head), dense_rows(sv, p, head))

    def nsa_block(i):
        s0 = i * Q_BLOCK
        pos_q = s0 + jnp.arange(Q_BLOCK)
        kw = lax.dynamic_slice_in_dim(kwp, s0, WINDOW + Q_BLOCK, axis=1)
        vw = lax.dynamic_slice_in_dim(vwp, s0, WINDOW + Q_BLOCK, axis=1)
        pos_w = s0 - WINDOW + jnp.arange(WINDOW + Q_BLOCK)
        return nsa_core(sl(qb, s0), sl(qbr, s0), sl(gb, s0), pos_q, kc, vc, overlap, gather_b, kw, vw, pos_w)

    o_b = unblock(lax.map(nsa_block, jnp.arange(n_blocks)))
    wb = min(WINDOW, S)
    return o_a, o_b, (ka, va, ik, ck, cv, sk, sv, wk[:, S - wb:], wv[:, S - wb:])


def mix_sample(parts, w_ck, w_cv, page_table, c_k, c_v, c_ik, c_ck, c_cv, c_sk, c_sv, s_wk, s_wv):
    qa, ka, va, iq, ik, iw, qb, qbr, ck, cv, sk, sv, wk, wv, gb = parts
    N, T = qa.shape[:2]
    past = page_table.shape[1] * c_k.shape[1]
    L = past + T
    pos_q = past + jnp.arange(T)
    full_past = lambda pool: pool[page_table].reshape((N, past) + pool.shape[2:])
    ik_all = jnp.concatenate([full_past(c_ik), ik], axis=1)
    gather_a = lambda idx: (paged_rows(c_k, page_table, ka, idx), paged_rows(c_v, page_table, va, idx))
    o_a = dsa_core(qa, iq, iw, pos_q, ik_all, min(DSA_TOPK, L // 4), gather_a)
    kc = compress(jnp.concatenate([full_past(c_ck), ck], axis=1), w_ck)
    vc = compress(jnp.concatenate([full_past(c_cv), cv], axis=1), w_cv)
    overlap = slc_overlap(kc.shape[1], L)
    head = jnp.arange(B_KV).reshape(1, 1, B_KV, 1)
    gather_b = lambda p: (paged_rows(c_sk, page_table, sk, p, head), paged_rows(c_sv, page_table, sv, p, head))
    kw = jnp.concatenate([s_wk, wk], axis=1)
    vw = jnp.concatenate([s_wv, wv], axis=1)
    wb = s_wk.shape[1]
    pos_w = past - wb + jnp.arange(wb + T)
    o_b = nsa_core(qb, qbr, gb, pos_q, kc, vc, overlap, gather_b, kw, vw, pos_w)
    return o_a, o_b, (ka, va, ik, ck, cv, sk, sv, kw[:, T:], vw[:, T:])


def trunk_layer(x, pos, mixer, g1, wu1, wd1, g2, w_in_l, w_pa, w_pb, w_o, g3, wu2, wd2):
    x = x + 0.5 * swiglu(rms_norm(x, g1), wu1, wd1)
    h = rms_norm(x, g2)
    parts, gate = split_proj(h, w_in_l, pos)
    o_a, o_b, state = mixer(parts)
    g_a, g_b = jnp.split(jax.nn.sigmoid(gate.astype(jnp.float32)).astype(x.dtype), 2, axis=-1)
    m = g_a * (o_a @ w_pa) + g_b * (o_b @ w_pb)
    x = x + m @ w_o
    x = x + 0.5 * swiglu(rms_norm(x, g3), wu2, wd2)
    return x, state


def stack_states(states, i):
    return jnp.stack([s[i] for s in states])


def setup_inputs(seed: int = 0) -> dict:
    key = jax.random.key(seed)
    ks = iter(jax.random.split(key, 40))
    nrm = lambda shape, scale=1.0: jax.random.normal(next(ks), shape, jnp.float32) * scale
    n_pages = PAST_LEN // PAGE_SIZE
    n_used = DEC_BATCH * n_pages
    n_pool = n_used + n_used // 4
    wb = min(WINDOW, PAST_LEN)
    page_table = jax.random.permutation(next(ks), n_pool)[:n_used].reshape(DEC_BATCH, n_pages).astype(jnp.int32)
    kv_a = (DEPTH, n_pool, PAGE_SIZE, A_KV, HEAD_DIM)
    kv_b = (DEPTH, n_pool, PAGE_SIZE, B_KV, HEAD_DIM)
    gain = lambda: 1.0 + nrm((DEPTH, D_MODEL), 0.02)
    return {
        "x_prompt": nrm((BATCH, SEQ, D_MODEL)),
        "x_sample": nrm((DEC_BATCH, DEC_SEQ, D_MODEL)),
        "cache_dsa_k": nrm(kv_a),
        "cache_dsa_v": nrm(kv_a),
        "cache_dsa_idx_k": nrm((DEPTH, n_pool, PAGE_SIZE, IDX_DIM)),
        "cache_nsa_cmp_k": nrm(kv_b),
        "cache_nsa_cmp_v": nrm(kv_b),
        "cache_nsa_slc_k": nrm(kv_b),
        "cache_nsa_slc_v": nrm(kv_b),
        "state_nsa_win_k": nrm((DEPTH, DEC_BATCH, wb, B_KV, HEAD_DIM)),
        "state_nsa_win_v": nrm((DEPTH, DEC_BATCH, wb, B_KV, HEAD_DIM)),
        "page_table": page_table,
        "norm_ffn1": gain(),
        "w_ffn1_up": nrm((DEPTH, D_MODEL, 2 * D_FF), D_MODEL ** -0.5),
        "w_ffn1_down": nrm((DEPTH, D_FF, D_MODEL), D_FF ** -0.5),
        "norm_mix": gain(),
        "w_in": nrm((DEPTH, D_MODEL, D_IN), D_MODEL ** -0.5),
        "w_cmp_k": nrm((DEPTH, CMP_LEN, B_KV, HEAD_DIM), CMP_LEN ** -0.5),
        "w_cmp_v": nrm((DEPTH, CMP_LEN, B_KV, HEAD_DIM), CMP_LEN ** -0.5),
        "w_proj_a": nrm((DEPTH, A_WIDTH, D_MODEL), A_WIDTH ** -0.5),
        "w_proj_b": nrm((DEPTH, B_WIDTH, D_MODEL), B_WIDTH ** -0.5),
        "w_out": nrm((DEPTH, D_MODEL, D_MODEL), D_MODEL ** -0.5),
        "norm_ffn2": gain(),
        "w_ffn2_up": nrm((DEPTH, D_MODEL, 2 * D_FF), D_MODEL ** -0.5),
        "w_ffn2_down": nrm((DEPTH, D_FF, D_MODEL), D_FF ** -0.5),
        "norm_final": 1.0 + nrm((D_MODEL,), 0.02),
    }


def reference(x_prompt, x_sample, cache_dsa_k, cache_dsa_v, cache_dsa_idx_k, cache_nsa_cmp_k, cache_nsa_cmp_v,
              cache_nsa_slc_k, cache_nsa_slc_v, state_nsa_win_k, state_nsa_win_v, page_table,
              norm_ffn1, w_ffn1_up, w_ffn1_down, norm_mix, w_in, w_cmp_k, w_cmp_v, w_proj_a, w_proj_b, w_out,
              norm_ffn2, w_ffn2_up, w_ffn2_down, norm_final):
    past = page_table.shape[1] * cache_dsa_k.shape[2]
    pos_p = jnp.arange(x_prompt.shape[1])
    pos_s = past + jnp.arange(x_sample.shape[1])
    xp, xs = x_prompt, x_sample
    st_p, st_s = [], []
    for l in range(DEPTH):
        lw = (norm_ffn1[l], w_ffn1_up[l], w_ffn1_down[l], norm_mix[l], w_in[l], w_proj_a[l], w_proj_b[l],
              w_out[l], norm_ffn2[l], w_ffn2_up[l], w_ffn2_down[l])
        mp = functools.partial(mix_prompt, w_ck=w_cmp_k[l], w_cv=w_cmp_v[l])
        ms = functools.partial(mix_sample, w_ck=w_cmp_k[l], w_cv=w_cmp_v[l], page_table=page_table,
                               c_k=cache_dsa_k[l], c_v=cache_dsa_v[l], c_ik=cache_dsa_idx_k[l],
                               c_ck=cache_nsa_cmp_k[l], c_cv=cache_nsa_cmp_v[l],
                               c_sk=cache_nsa_slc_k[l], c_sv=cache_nsa_slc_v[l],
                               s_wk=state_nsa_win_k[l], s_wv=state_nsa_win_v[l])
        xp, sp = trunk_layer(xp, pos_p, mp, *lw)
        xs, ss = trunk_layer(xs, pos_s, ms, *lw)
        st_p.append(sp)
        st_s.append(ss)
    y_prompt = rms_norm(xp, norm_final)
    y_sample = rms_norm(xs, norm_final)
    dsa_k_p, dsa_k_s = stack_states(st_p, 0), stack_states(st_s, 0)
    dsa_v_p, dsa_v_s = stack_states(st_p, 1), stack_states(st_s, 1)
    dsa_idx_k_p, dsa_idx_k_s = stack_states(st_p, 2), stack_states(st_s, 2)
    nsa_cmp_k_p, nsa_cmp_k_s = stack_states(st_p, 3), stack_states(st_s, 3)
    nsa_cmp_v_p, nsa_cmp_v_s = stack_states(st_p, 4), stack_states(st_s, 4)
    nsa_slc_k_p, nsa_slc_k_s = stack_states(st_p, 5), stack_states(st_s, 5)
    nsa_slc_v_p, nsa_slc_v_s = stack_states(st_p, 6), stack_states(st_s, 6)
    nsa_win_k_p, nsa_win_k_s = stack_states(st_p, 7), stack_states(st_s, 7)
    nsa_win_v_p, nsa_win_v_s = stack_states(st_p, 8), stack_states(st_s, 8)
    return (y_prompt, y_sample, dsa_k_p, dsa_k_s, dsa_v_p, dsa_v_s, dsa_idx_k_p, dsa_idx_k_s,
            nsa_cmp_k_p, nsa_cmp_k_s, nsa_cmp_v_p, nsa_cmp_v_s, nsa_slc_k_p, nsa_slc_k_s,
            nsa_slc_v_p, nsa_slc_v_s, nsa_win_k_p, nsa_win_k_s, nsa_win_v_p, nsa_win_v_s)
```

```python
import functools

import jax
import jax.numpy as jnp
import numpy as np
from jax import lax
from jax.experimental import pallas as pl
from jax.experimental.pallas import tpu as pltpu

D_MODEL = 1024
HEAD_DIM = 64
ROT_DIM = HEAD_DIM // 4
ROPE_THETA = 500000.0
A_KV = 2
IDX_HEADS = 8
IDX_DIM = 64
DSA_TOPK = 256
B_HEADS = 8
B_KV = 2
CMP_STRIDE = 16
CMP_LEN = 2 * CMP_STRIDE
SLC_BLK = 64
SLC_TOPN = 16
WINDOW = 512
D_FF = 2816
Q_BLOCK = 128
EPS = 1e-6
NEG = -1e30
INT_MIN = -(2 ** 31)

LANES = 128
VMEM_LIMIT = 56 * 1024 * 1024

F32 = jnp.float32
BF16 = jnp.bfloat16

C_QA, C_KA, C_VA, C_IQ, C_IKW, C_QB = 0, 512, 640, 768, 1280, 1408
C_CK, C_CV, C_SK, C_SV, C_WK, C_WV, C_GB, C_GATE = 1920, 2048, 2176, 2304, 2432, 2560, 2688, 2816
D_IN_ALIGNED = C_GATE + 2 * D_MODEL


def _cparams(sem):
    return pltpu.CompilerParams(dimension_semantics=sem, vmem_limit_bytes=VMEM_LIMIT)


def _rms(x, g):
    return x * lax.rsqrt(jnp.mean(x * x, axis=-1, keepdims=True) + EPS) * g


def _dot(a, b):
    return jnp.dot(a, b, preferred_element_type=F32)


def _dot_nt(a, b):
    return lax.dot_general(a, b, (((1,), (1,)), ((), ())), preferred_element_type=F32)


def _loop(n, body, init):
    if isinstance(n, int):
        carry = init
        for j in range(n):
            carry = body(j, carry)
        return carry
    return lax.fori_loop(0, n, body, init)


def _ffn_kernel(x_ref, g_ref, wa_ref, wb_ref, wd_ref, gf_ref, o_ref, h_s, acc_s, *, n_ff, final):
    j = pl.program_id(1)

    @pl.when(j == 0)
    def _():
        h_s[...] = _rms(x_ref[...], g_ref[...]).astype(BF16)
        acc_s[...] = jnp.zeros_like(acc_s)

    h = h_s[...]
    a = _dot(h, wa_ref[...])
    b = _dot(h, wb_ref[...])
    act = (a * jax.nn.sigmoid(a)) * b
    acc_s[...] += _dot(act.astype(BF16), wd_ref[...])

    @pl.when(j == n_ff - 1)
    def _():
        y = x_ref[...] + 0.5 * acc_s[...]
        if final:
            y = _rms(y, gf_ref[...])
        o_ref[...] = y


def _ffn(x, g, w_up, w_down, g_final, *, tm, tf, final):
    T = x.shape[0]
    n_ff = D_FF // tf
    return pl.pallas_call(
        functools.partial(_ffn_kernel, n_ff=n_ff, final=final),
        grid=(T // tm, n_ff),
        in_specs=[
            pl.BlockSpec((tm, D_MODEL), lambda i, j: (i, 0)),
            pl.BlockSpec((1, D_MODEL), lambda i, j: (0, 0)),
            pl.BlockSpec((D_MODEL, tf), lambda i, j: (0, j)),
            pl.BlockSpec((D_MODEL, tf), lambda i, j: (0, j + n_ff)),
            pl.BlockSpec((tf, D_MODEL), lambda i, j: (j, 0)),
            pl.BlockSpec((1, D_MODEL), lambda i, j: (0, 0)),
        ],
        out_specs=pl.BlockSpec((tm, D_MODEL), lambda i, j: (i, 0)),
        out_shape=jax.ShapeDtypeStruct((T, D_MODEL), F32),
        scratch_shapes=[pltpu.VMEM((tm, D_MODEL), BF16), pltpu.VMEM((tm, D_MODEL), F32)],
        compiler_params=_cparams(("arbitrary", "arbitrary")),
        name="ffn",
    )(x, g, w_up, w_up, w_down, g_final)


def _rope_tables(pos):
    half = ROT_DIM // 2
    inv = ROPE_THETA ** (-jnp.arange(half, dtype=F32) / half)
    ang = pos.astype(F32)[:, None] * inv
    cos, sin = jnp.cos(ang), jnp.sin(ang)
    n = pos.shape[0]
    one, zero = jnp.ones((n, HEAD_DIM - ROT_DIM), F32), jnp.zeros((n, HEAD_DIM - ROT_DIM), F32)
    z8 = jnp.zeros((n, half), F32)
    c = jnp.concatenate([cos, cos, one], axis=1)
    slo = jnp.concatenate([-sin, z8, zero], axis=1)
    shi = jnp.concatenate([z8, sin, zero], axis=1)
    return tuple(jnp.concatenate([t, t], axis=1) for t in (c, slo, shi))


_PROJ_TOKEN_OUTS = ("qa", "iq", "qb", "qbr", "ikw", "ck", "cv", "gb", "gate")
_PROJ_ROW_OUTS = ("ka", "va", "sk", "sv", "wk", "wv")


def _proj_kernel(x_ref, g_ref, w_ref, c_ref, slo_ref, shi_ref, *out_refs, channel_major):
    names = _PROJ_TOKEN_OUTS + (("ikT", "ckT", "cvT") if channel_major else ()) + _PROJ_ROW_OUTS
    o = dict(zip(names, out_refs))
    h = _rms(x_ref[...], g_ref[...]).astype(BF16)
    C, SLO, SHI = c_ref[...], slo_ref[...], shi_ref[...]
    scale = HEAD_DIM ** -0.5

    def mm(c0, width):
        return _dot(h, w_ref[:, c0:c0 + width])

    def rope(p):
        outs = []
        for j in range(p.shape[1] // LANES):
            xj = p[:, j * LANES:(j + 1) * LANES]
            outs.append(xj * C + pltpu.roll(xj, LANES - ROT_DIM // 2, 1) * SLO
                        + pltpu.roll(xj, ROT_DIM // 2, 1) * SHI)
        return outs[0] if len(outs) == 1 else jnp.concatenate(outs, axis=1)

    def put_rows(name, v):
        if channel_major:
            o[name][0] = v.T
        else:
            o[name][...] = v

    o["qa"][...] = (rope(mm(C_QA, 512)) * scale).astype(BF16)
    o["iq"][...] = (rope(mm(C_IQ, 512)) * (IDX_DIM ** -0.5)).astype(BF16)
    qb = mm(C_QB, 512)
    o["qb"][...] = (qb * scale).astype(BF16)
    o["qbr"][...] = (rope(qb) * scale).astype(BF16)
    ikw = mm(C_IKW, 128)
    lane = lax.broadcasted_iota(jnp.int32, ikw.shape, 1)
    ikw = jnp.where(lane < IDX_DIM, rope(ikw), ikw)
    o["ikw"][...] = ikw
    ck, cv = mm(C_CK, 128), mm(C_CV, 128)
    o["ck"][...] = ck
    o["cv"][...] = cv
    if channel_major:
        o["ikT"][0] = ikw.T[0:IDX_DIM, :]
        o["ckT"][0] = ck.T
        o["cvT"][0] = cv.T
    o["gb"][...] = jax.nn.sigmoid(mm(C_GB, 128))
    o["gate"][...] = jax.nn.sigmoid(mm(C_GATE, 2 * D_MODEL))
    put_rows("ka", rope(mm(C_KA, 128)))
    put_rows("va", mm(C_VA, 128))
    put_rows("sk", rope(mm(C_SK, 128)))
    put_rows("sv", mm(C_SV, 128))
    put_rows("wk", rope(mm(C_WK, 128)))
    put_rows("wv", mm(C_WV, 128))


def _proj(x, g, w_al, tables, *, tm, seq, channel_major):
    T = x.shape[0]
    per_seq = seq // tm
    row = lambda i: (i, 0)
    fixed = lambda i: (0, 0)
    tab = pl.BlockSpec((tm, LANES), lambda i: (i % per_seq, 0))
    widths = {"qa": 512, "iq": 512, "qb": 512, "qbr": 512, "gate": 2 * D_MODEL}
    shapes, specs = [], []
    for name in _PROJ_TOKEN_OUTS:
        w = widths.get(name, LANES)
        shapes.append(jax.ShapeDtypeStruct((T, w), BF16 if w == 512 else F32))
        specs.append(pl.BlockSpec((tm, w), row))
    cm = lambda ch: (jax.ShapeDtypeStruct((T // seq, ch, seq), F32),
                     pl.BlockSpec((1, ch, tm), lambda i: (i // per_seq, 0, i % per_seq)))
    names = _PROJ_TOKEN_OUTS
    if channel_major:
        names = names + ("ikT", "ckT", "cvT") + _PROJ_ROW_OUTS
        for ch in (IDX_DIM,) + (LANES,) * 8:
            sh, sp = cm(ch)
            shapes.append(sh)
            specs.append(sp)
    else:
        names = names + _PROJ_ROW_OUTS
        for _ in _PROJ_ROW_OUTS:
            shapes.append(jax.ShapeDtypeStruct((T, LANES), F32))
            specs.append(pl.BlockSpec((tm, LANES), row))
    outs = pl.pallas_call(
        functools.partial(_proj_kernel, channel_major=channel_major),
        grid=(T // tm,),
        in_specs=[pl.BlockSpec((tm, D_MODEL), row), pl.BlockSpec((1, D_MODEL), fixed),
                  pl.BlockSpec((D_MODEL, D_IN_ALIGNED), fixed), tab, tab, tab],
        out_specs=specs,
        out_shape=shapes,
        compiler_params=_cparams(("arbitrary",)),
        name="proj",
    )(x, g, w_al, *tables)
    return dict(zip(names, outs))


def _sort_key(score):
    bits = lax.bitcast_convert_type(score, jnp.int32)
    key = jnp.where(bits < 0, bits ^ jnp.int32(0x7FFFFFFF), bits)
    return jnp.where(score == 0.0, jnp.int32(0), key)


def _fold_lanes(x):
    acc = x[:, :LANES]
    for j in range(1, x.shape[1] // LANES):
        acc = acc + x[:, j * LANES:(j + 1) * LANES]
    return acc


def _count(get_keys, nch, rows, pred):
    def body(c, acc):
        return acc + _fold_lanes(jnp.where(pred(get_keys(c)), 1.0, 0.0))
    acc = _loop(nch, body, jnp.zeros((rows, LANES), F32))
    return jnp.sum(acc, axis=-1, keepdims=True)


def _kth_largest(get_keys, nch, k, rows):
    def bit_body(b, t):
        trial = t + lax.shift_left(jnp.int32(1), jnp.int32(31) - b)
        cnt = _count(get_keys, nch, rows, lambda kc: kc >= trial)
        return jnp.where(cnt >= k, trial, t)
    return lax.fori_loop(0, 32, bit_body, jnp.full((rows, 1), INT_MIN, jnp.int32))


def _prefix_matrix():
    r = lax.broadcasted_iota(jnp.int32, (LANES, LANES), 0)
    c = lax.broadcasted_iota(jnp.int32, (LANES, LANES), 1)
    return jnp.where(r <= c, 1.0, 0.0).astype(BF16)


def _topk_select(get_keys, put_sel, nch, k, rows, ch):
    t = _kth_largest(get_keys, nch, k, rows)
    need = k - _count(get_keys, nch, rows, lambda kc: kc > t)
    tri = _prefix_matrix()

    def body(c, run):
        kc = get_keys(c)
        sels = []
        for j in range(ch // LANES):
            kj = kc[:, j * LANES:(j + 1) * LANES]
            eq = kj == t
            pj = _dot(jnp.where(eq, 1.0, 0.0).astype(BF16), tri)
            take = (kj > t) | (eq & ((pj + run) <= need))
            sels.append(jnp.where(take & (kj != INT_MIN), 1.0, 0.0))
            run = run + pj[:, LANES - 1:LANES]
        put_sel(c, sels[0] if len(sels) == 1 else jnp.concatenate(sels, axis=1))
        return run

    _loop(nch, body, jnp.zeros((rows, 1), F32))


def _attn_online(q, get_k, get_v, get_mask, nch):
    m_rows = q.shape[0]

    def body(c, carry):
        m, l, acc = carry
        s = _dot(q, get_k(c))
        keep = jnp.broadcast_to(get_mask(c), s.shape) > 0.5
        s = jnp.where(keep, s, NEG)
        m_new = jnp.maximum(m, jnp.max(s, axis=-1, keepdims=True))
        alpha = jnp.exp(m - m_new)
        p = jnp.where(keep, jnp.exp(s - m_new), 0.0)
        l = alpha * l + jnp.sum(p, axis=-1, keepdims=True)
        acc = alpha * acc + _dot_nt(p.astype(BF16), get_v(c))
        return m_new, l, acc

    init = (jnp.full((m_rows, 1), NEG, F32), jnp.zeros((m_rows, 1), F32), jnp.zeros((m_rows, LANES), F32))
    _, l, acc = _loop(nch, body, init)
    return acc / jnp.where(l > 0.0, l, 1.0)


def _softmax_once(s, keep):
    s = jnp.where(keep, s, NEG)
    e = jnp.where(keep, jnp.exp(s - jnp.max(s, axis=-1, keepdims=True)), 0.0)
    l = jnp.sum(e, axis=-1, keepdims=True)
    return e / jnp.where(l > 0.0, l, 1.0)


def _dot_f32_exact_rhs(p, w_bf16):
    p1 = p.astype(BF16)
    r1 = p - p1.astype(F32)
    p2 = r1.astype(BF16)
    p3 = (r1 - p2.astype(F32)).astype(BF16)
    return _dot(p1, w_bf16) + _dot(p2, w_bf16) + _dot(p3, w_bf16)


def _group_lanes(shape, g):
    return (lax.broadcasted_iota(jnp.int32, shape, 1) // HEAD_DIM) == g


def _q_rows(q32, g):
    parts = []
    for r in range(4):
        h = 4 * g + r
        slab = q32[:, (h // 2) * LANES:(h // 2 + 1) * LANES]
        if h % 2 != g:
            slab = pltpu.roll(slab, HEAD_DIM, 1)
        parts.append(jnp.where(_group_lanes(slab.shape, g), slab, 0.0))
    return jnp.concatenate(parts, axis=0).astype(BF16)


def _heads_to_slabs(o_groups, rows):
    slabs = []
    for j in range(4):
        g, r0 = j // 2, (2 * j) % 4
        a = o_groups[g][r0 * rows:(r0 + 1) * rows]
        b = o_groups[g][(r0 + 1) * rows:(r0 + 2) * rows]
        if g == 1:
            a = pltpu.roll(a, HEAD_DIM, 1)
        else:
            b = pltpu.roll(b, HEAD_DIM, 1)
        lane = lax.broadcasted_iota(jnp.int32, a.shape, 1)
        slabs.append(jnp.where(lane < HEAD_DIM, a, b))
    return jnp.concatenate(slabs, axis=1)


def _tile_rows(x, h):
    return jnp.broadcast_to(x[None], (h,) + x.shape).reshape(h * x.shape[0], x.shape[1])


P_CH = 512


def _dsa_prompt_kernel(iq_ref, qa_ref, ikw_ref, ikT_ref, kT_ref, vT_ref, o_ref,
                       ik_s, k_s, v_s, iqp_s, keys_s, sel_s, *, k_sel, n_chunks):
    i = pl.program_id(1)
    R = Q_BLOCK

    @pl.when(i == 0)
    def _():
        for c in range(n_chunks):
            cols = slice(c * P_CH, (c + 1) * P_CH)
            ik_s[c] = ikT_ref[0, :, cols].astype(BF16)
            k_s[c] = kT_ref[0, :, cols].astype(BF16)
            v_s[c] = vT_ref[0, :, cols].astype(BF16)

    nch = lax.shift_right_logical(i, 2) + 1
    qpos = i * R + lax.broadcasted_iota(jnp.int32, (R, 1), 0)

    iq32 = iq_ref[...].astype(F32)
    for h in range(IDX_HEADS):
        iqp_s[h] = iq32[:, h * IDX_DIM:(h + 1) * IDX_DIM].astype(BF16)
    w128 = ikw_ref[...] * (IDX_HEADS ** -0.5)
    wcol = [w128[:, IDX_DIM + h:IDX_DIM + h + 1] for h in range(IDX_HEADS)]

    def score_body(c, carry):
        ikc = ik_s[c]
        tot = jnp.zeros((R, P_CH), F32)
        for h in range(IDX_HEADS):
            tot = tot + wcol[h] * jnp.maximum(_dot(iqp_s[h], ikc), 0.0)
        kpos = c * P_CH + lax.broadcasted_iota(jnp.int32, (R, P_CH), 1)
        keys_s[c] = jnp.where(kpos <= qpos, _sort_key(tot), INT_MIN)
        return carry

    lax.fori_loop(0, nch, score_body, 0)

    def put_sel(c, v):
        sel_s[c] = v

    _topk_select(lambda c: keys_s[c], put_sel, nch, float(k_sel), R, P_CH)

    qa32 = qa_ref[...].astype(F32)
    outs = []
    for g in range(A_KV):
        outs.append(_attn_online(_q_rows(qa32, g), lambda c: k_s[c], lambda c: v_s[c],
                                 lambda c: _tile_rows(sel_s[c], 4), nch))
    o_ref[...] = _heads_to_slabs(outs, R).astype(o_ref.dtype)


def _dsa_prompt(iq, qa, ikw, ikT, kT, vT, *, n, s):
    nq = s // Q_BLOCK
    n_chunks = s // P_CH
    k_sel = min(DSA_TOPK, s // 4)
    qrow = lambda b, i: (b * nq + i, 0)
    full = lambda b, i: (b, 0, 0)
    return pl.pallas_call(
        functools.partial(_dsa_prompt_kernel, k_sel=k_sel, n_chunks=n_chunks),
        grid=(n, nq),
        in_specs=[pl.BlockSpec((Q_BLOCK, 512), qrow), pl.BlockSpec((Q_BLOCK, 512), qrow),
                  pl.BlockSpec((Q_BLOCK, LANES), qrow),
                  pl.BlockSpec((1, IDX_DIM, s), full), pl.BlockSpec((1, LANES, s), full),
                  pl.BlockSpec((1, LANES, s), full)],
        out_specs=pl.BlockSpec((Q_BLOCK, 512), qrow),
        out_shape=jax.ShapeDtypeStruct((n * s, 512), BF16),
        scratch_shapes=[pltpu.VMEM((n_chunks, IDX_DIM, P_CH), BF16), pltpu.VMEM((n_chunks, LANES, P_CH), BF16),
                        pltpu.VMEM((n_chunks, LANES, P_CH), BF16),
                        pltpu.VMEM((IDX_HEADS, Q_BLOCK, IDX_DIM), BF16),
                        pltpu.VMEM((n_chunks, Q_BLOCK, P_CH), jnp.int32),
                        pltpu.VMEM((n_chunks, Q_BLOCK, P_CH), F32)],
        compiler_params=_cparams(("arbitrary", "arbitrary")),
        name="dsa_prompt",
    )(iq, qa, ikw, ikT, kT, vT)


def _compress_rows(src_ref, w_ref, n_chunks):
    lo = jnp.zeros((n_chunks, LANES), F32)
    hi = jnp.zeros((n_chunks, LANES), F32)
    for i in range(CMP_STRIDE):
        xi = src_ref[pl.ds(i, n_chunks, stride=CMP_STRIDE), :]
        lo = lo + xi * w_ref[i:i + 1, :]
        hi = hi + xi * w_ref[CMP_STRIDE + i:CMP_STRIDE + i + 1, :]
    return lo + pltpu.roll(hi, n_chunks - 1, 0)


def _compress_kernel(ck_ref, cv_ref, wk_ref, wv_ref, kc_o, vc_o, *, n_chunks):
    kc_o[...] = _compress_rows(ck_ref, wk_ref, n_chunks)
    vc_o[...] = _compress_rows(cv_ref, wv_ref, n_chunks)


def _compress_prompt(ck, cv, w_ck, w_cv, *, n, s):
    n_chunks = s // CMP_STRIDE
    full = lambda b: (b, 0)
    fixed = lambda b: (0, 0)
    shape = jax.ShapeDtypeStruct((n * n_chunks, LANES), F32)
    return pl.pallas_call(
        functools.partial(_compress_kernel, n_chunks=n_chunks),
        grid=(n,),
        in_specs=[pl.BlockSpec((s, LANES), full), pl.BlockSpec((s, LANES), full),
                  pl.BlockSpec((CMP_LEN, LANES), fixed), pl.BlockSpec((CMP_LEN, LANES), fixed)],
        out_specs=[pl.BlockSpec((n_chunks, LANES), full)] * 2,
        out_shape=[shape, shape],
        compiler_params=_cparams(("arbitrary",)),
        name="compress_prompt",
    )(ck, cv, w_ck, w_cv)


def _slc_overlap(n_cmp, n_keys, rows, cols):
    n_slc = -(-n_keys // SLC_BLK)
    j = np.arange(n_cmp)[:, None]
    sb = np.arange(n_slc)[None, :]
    lo = np.maximum(j * CMP_STRIDE, sb * SLC_BLK)
    hi = np.minimum(j * CMP_STRIDE + CMP_LEN, (sb + 1) * SLC_BLK)
    ov = np.zeros((rows, cols), np.float32)
    ov[:n_cmp, :n_slc] = np.maximum(hi - lo, 0) / CMP_STRIDE
    return jnp.asarray(ov, dtype=BF16)


def _block_keys(p_s, qpos, n_slc):
    blk = lax.broadcasted_iota(jnp.int32, p_s.shape, 1)
    cur = lax.shift_right_logical(qpos, 6)
    forced = (blk == 0) | (blk == cur) | (blk == cur - 1)
    p_s = jnp.where(forced, 1e6, p_s)
    p_s = jnp.where(blk * SLC_BLK <= qpos, p_s, NEG)
    return jnp.where(blk < n_slc, _sort_key(p_s), INT_MIN)


def _expand_blocks(sel_b, c, ch):
    nb = sel_b.shape[1]
    blk = lax.broadcasted_iota(jnp.int32, (nb, ch), 0)
    kblk = lax.shift_right_logical(c * ch + lax.broadcasted_iota(jnp.int32, (nb, ch), 1), 6)
    e = jnp.where(blk == kblk, 1.0, 0.0).astype(BF16)
    return _dot(sel_b.astype(BF16), e)


def _nsa_prompt_kernel(qb_ref, qbr_ref, gb_ref, kc_ref, vc_ref, ov_ref, skT_ref, svT_ref, wkT_ref, wvT_ref, o_ref,
                       kc_s, vc_s, sk_s, sv_s, wk_s, wv_s, *, s_len, n_cmp):
    i = pl.program_id(1)
    R = Q_BLOCK
    n_cmp_rows = s_len // CMP_STRIDE
    n_slc = s_len // SLC_BLK
    w_pad = WINDOW // R
    w_tiles = WINDOW // R + 1

    @pl.when(i == 0)
    def _():
        kc_s[...] = kc_ref[...].astype(BF16)
        vc_s[...] = vc_ref[...].astype(BF16)
        for c in range(s_len // P_CH):
            cols = slice(c * P_CH, (c + 1) * P_CH)
            sk_s[c] = skT_ref[0, :, cols].astype(BF16)
            sv_s[c] = svT_ref[0, :, cols].astype(BF16)
        for c in range(w_pad):
            wk_s[c] = jnp.zeros((LANES, R), BF16)
            wv_s[c] = jnp.zeros((LANES, R), BF16)
        for c in range(s_len // R):
            cols = slice(c * R, (c + 1) * R)
            wk_s[w_pad + c] = wkT_ref[0, :, cols].astype(BF16)
            wv_s[w_pad + c] = wvT_ref[0, :, cols].astype(BF16)

    nch = lax.shift_right_logical(i, 2) + 1
    qpos = i * R + lax.broadcasted_iota(jnp.int32, (R, 1), 0)
    qb32 = qb_ref[...].astype(F32)
    qbr32 = qbr_ref[...].astype(F32)
    gsig = gb_ref[...]

    cidx = lax.broadcasted_iota(jnp.int32, (R, n_cmp_rows), 1)
    vis_c = jnp.where((cidx * CMP_STRIDE + CMP_LEN - 1 <= qpos) & (cidx < n_cmp), 1.0, 0.0)
    pos_w = i * R - WINDOW + lax.broadcasted_iota(jnp.int32, (R, WINDOW + R), 1)
    dist = qpos - pos_w
    vis_w = jnp.where((pos_w >= 0) & (dist >= 0) & (dist <= WINDOW), 1.0, 0.0)
    kw = jnp.concatenate([wk_s[i + t] for t in range(w_tiles)], axis=1)
    vw = jnp.concatenate([wv_s[i + t] for t in range(w_tiles)], axis=1)

    outs = []
    for g in range(B_KV):
        q = _q_rows(qb32, g)
        qr = _q_rows(qbr32, g)
        p_c = _softmax_once(_dot_nt(q, kc_s[...]), _tile_rows(vis_c, 4) > 0.5)
        o_c = _dot(p_c.astype(BF16), vc_s[...])
        p_sum = p_c[0:R] + p_c[R:2 * R] + p_c[2 * R:3 * R] + p_c[3 * R:4 * R]
        keys = _block_keys(_dot_f32_exact_rhs(p_sum, ov_ref[...]), qpos, n_slc)
        sel_box = []
        _topk_select(lambda c: keys, lambda c, v: sel_box.append(v), 1, float(min(SLC_TOPN, n_slc)), R, LANES)
        sel_b = sel_box[0]

        def slc_mask(c):
            kpos = c * P_CH + lax.broadcasted_iota(jnp.int32, (R, P_CH), 1)
            m = jnp.where(kpos <= qpos, _expand_blocks(sel_b, c, P_CH), 0.0)
            return _tile_rows(m, 4)

        o_s = _attn_online(qr, lambda c: sk_s[c], lambda c: sv_s[c], slc_mask, nch)
        p_w = _softmax_once(_dot(qr, kw), _tile_rows(vis_w, 4) > 0.5)
        o_w = _dot_nt(p_w.astype(BF16), vw)
        mixed = []
        for r in range(4):
            h = 4 * g + r
            rows = slice(r * R, (r + 1) * R)
            mixed.append(gsig[:, 3 * h:3 * h + 1] * o_c[rows] + gsig[:, 3 * h + 1:3 * h + 2] * o_s[rows]
                         + gsig[:, 3 * h + 2:3 * h + 3] * o_w[rows])
        outs.append(jnp.concatenate(mixed, axis=0))
    o_ref[...] = _heads_to_slabs(outs, R).astype(o_ref.dtype)


def _nsa_prompt(qb, qbr, gb, kc, vc, ov, skT, svT, wkT, wvT, *, n, s):
    nq = s // Q_BLOCK
    n_cmp_rows = s // CMP_STRIDE
    n_cmp = (s - CMP_LEN) // CMP_STRIDE + 1
    qrow = lambda b, i: (b * nq + i, 0)
    per_b = lambda b, i: (b, 0)
    full = lambda b, i: (b, 0, 0)
    fixed = lambda b, i: (0, 0)
    return pl.pallas_call(
        functools.partial(_nsa_prompt_kernel, s_len=s, n_cmp=n_cmp),
        grid=(n, nq),
        in_specs=[pl.BlockSpec((Q_BLOCK, 512), qrow), pl.BlockSpec((Q_BLOCK, 512), qrow),
                  pl.BlockSpec((Q_BLOCK, LANES), qrow),
                  pl.BlockSpec((n_cmp_rows, LANES), per_b), pl.BlockSpec((n_cmp_rows, LANES), per_b),
                  pl.BlockSpec((n_cmp_rows, LANES), fixed)]
        + [pl.BlockSpec((1, LANES, s), full)] * 4,
        out_specs=pl.BlockSpec((Q_BLOCK, 512), qrow),
        out_shape=jax.ShapeDtypeStruct((n * s, 512), BF16),
        scratch_shapes=[pltpu.VMEM((n_cmp_rows, LANES), BF16)] * 2
        + [pltpu.VMEM((s // P_CH, LANES, P_CH), BF16)] * 2
        + [pltpu.VMEM(((WINDOW + s) // Q_BLOCK, LANES, Q_BLOCK), BF16)] * 2,
        compiler_params=_cparams(("arbitrary", "arbitrary")),
        name="nsa_prompt",
    )(qb, qbr, gb, kc, vc, ov, skT, svT, wkT, wvT)


S_CH = 640
S_ROWS = 16


def _page_dst(buf, p, page):
    c, off = divmod(p * page, S_CH)
    return c, off


def _start_all(copies):
    for cp in copies:
        cp.start()
    return copies


def _page_copies_chunked(pt_ref, b, layer, pool, buf, sem, n_pages, page, grouped):
    copies = []
    for p in range(n_pages):
        pg = pt_ref[b, p]
        c, off = _page_dst(buf, p, page)
        if grouped:
            for g in range(2):
                copies.append(pltpu.make_async_copy(
                    pool.at[layer, pg, g], buf.at[c, pl.ds(g * HEAD_DIM, HEAD_DIM), pl.ds(off, page)], sem))
        else:
            copies.append(pltpu.make_async_copy(pool.at[layer, pg], buf.at[c, :, pl.ds(off, page)], sem))
    return copies


def _page_copies_flat(pt_ref, b, layer, pool, buf, sem, n_pages, page):
    copies = []
    for p in range(n_pages):
        pg = pt_ref[b, p]
        for g in range(2):
            copies.append(pltpu.make_async_copy(
                pool.at[layer, pg, g], buf.at[pl.ds(g * HEAD_DIM, HEAD_DIM), pl.ds(p * page, page)], sem))
    return copies


def _set_new_column(buf, past, col):
    c, off = divmod(past, S_CH)
    ch = buf.shape[1]
    lane = lax.broadcasted_iota(jnp.int32, (ch, LANES), 1)
    buf[c, :, off:off + LANES] = jnp.where(lane == 0, jnp.broadcast_to(col, (ch, LANES)), 0.0)


def _pad_heads(q8):
    row = lax.broadcasted_iota(jnp.int32, q8.shape, 0)
    lane = lax.broadcasted_iota(jnp.int32, q8.shape, 1)
    q8 = jnp.where((row // 4) == (lane // HEAD_DIM), q8, 0.0)
    return jnp.concatenate([q8, jnp.zeros_like(q8)], axis=0).astype(BF16)


def _dsa_sample_kernel(pt_ref, iq_ref, iw_ref, qa_ref, ikn_ref, kan_ref, van_ref, cik_hbm, ck_hbm, cv_hbm,
                       o_ref, ik_b, k_b, v_b, keys_s, sel_s, sems, *, layer, past, n_pages, page, k_sel):
    b = pl.program_id(0)
    copies = _start_all(
        _page_copies_chunked(pt_ref, b, layer, cik_hbm, ik_b, sems.at[0], n_pages, page, False)
        + _page_copies_chunked(pt_ref, b, layer, ck_hbm, k_b, sems.at[1], n_pages, page, True)
        + _page_copies_chunked(pt_ref, b, layer, cv_hbm, v_b, sems.at[2], n_pages, page, True))
    _set_new_column(ik_b, past, ikn_ref[0][0:IDX_DIM, :])
    _set_new_column(k_b, past, kan_ref[0])
    _set_new_column(v_b, past, van_ref[0])
    for cp in copies:
        cp.wait()

    nch = ik_b.shape[0]
    iq16 = jnp.concatenate([iq_ref[0], jnp.zeros((8, IDX_DIM), F32)], axis=0).astype(BF16)
    w16 = jnp.concatenate([iw_ref[0] * (IDX_HEADS ** -0.5), jnp.zeros((8, 1), F32)], axis=0)

    def score_body(c, carry):
        d = jnp.maximum(_dot(iq16, ik_b[c].astype(BF16)), 0.0)
        tot = jnp.sum(w16 * d, axis=0, keepdims=True)
        kpos = c * S_CH + lax.broadcasted_iota(jnp.int32, (1, S_CH), 1)
        key = jnp.where(kpos <= past, _sort_key(tot), INT_MIN)
        keys_s[c] = jnp.broadcast_to(key, (8, S_CH))
        return carry

    lax.fori_loop(0, nch, score_body, 0)

    def put_sel(c, v):
        sel_s[c] = v

    _topk_select(lambda c: keys_s[c], put_sel, nch, float(k_sel), 8, S_CH)
    o = _attn_online(_pad_heads(qa_ref[0]), lambda c: k_b[c].astype(BF16), lambda c: v_b[c].astype(BF16),
                     lambda c: sel_s[c][0:1, :], nch)
    o_ref[0] = o[0:8]


def _dsa_sample(page_table, iq, iw, qa, ikn, kan, van, c_ik, c_k, c_v, *, layer, past):
    n = page_table.shape[0]
    n_pages = page_table.shape[1]
    page = c_k.shape[-1]
    nch = (past + LANES) // S_CH
    per_seq = lambda width, lanes: pl.BlockSpec((1, width, lanes), lambda b, pt: (b, 0, 0))
    anyspec = pl.BlockSpec(memory_space=pl.ANY)
    grid_spec = pltpu.PrefetchScalarGridSpec(
        num_scalar_prefetch=1,
        grid=(n,),
        in_specs=[per_seq(8, IDX_DIM), per_seq(8, 1), per_seq(8, LANES),
                  per_seq(LANES, 1), per_seq(LANES, 1), per_seq(LANES, 1), anyspec, anyspec, anyspec],
        out_specs=per_seq(8, LANES),
        scratch_shapes=[pltpu.VMEM((nch, IDX_DIM, S_CH), F32), pltpu.VMEM((nch, LANES, S_CH), F32),
                        pltpu.VMEM((nch, LANES, S_CH), F32),
                        pltpu.VMEM((nch, 8, S_CH), jnp.int32), pltpu.VMEM((nch, 8, S_CH), F32),
                        pltpu.SemaphoreType.DMA((3,))],
    )
    return pl.pallas_call(
        functools.partial(_dsa_sample_kernel, layer=layer, past=past, n_pages=n_pages, page=page,
                          k_sel=min(DSA_TOPK, (past + 1) // 4)),
        grid_spec=grid_spec,
        out_shape=jax.ShapeDtypeStruct((n, 8, LANES), F32),
        compiler_params=_cparams(("arbitrary",)),
        name="dsa_sample",
    )(page_table, iq, iw, qa, ikn, kan, van, c_ik, c_k, c_v)


def _nsa_sample_kernel(pt_ref, qb_ref, qbr_ref, g_ref, skn_ref, svn_ref, wkn_ref, wvn_ref, wcmpk_ref, wcmpv_ref,
                       ov_ref, swk_ref, swv_ref, cck_hbm, ccv_hbm, csk_hbm, csv_hbm,
                       o_ref, nwk_ref, nwv_ref, ckT_b, cvT_b, ck_b, cv_b, sk_b, sv_b, sems,
                       *, layer, past, n_pages, page, n_cmp):
    b = pl.program_id(0)
    copies = _start_all(
        _page_copies_flat(pt_ref, b, layer, cck_hbm, ckT_b, sems.at[0], n_pages, page)
        + _page_copies_flat(pt_ref, b, layer, ccv_hbm, cvT_b, sems.at[1], n_pages, page)
        + _page_copies_chunked(pt_ref, b, layer, csk_hbm, sk_b, sems.at[2], n_pages, page, True)
        + _page_copies_chunked(pt_ref, b, layer, csv_hbm, sv_b, sems.at[3], n_pages, page, True))
    _set_new_column(sk_b, past, skn_ref[0])
    _set_new_column(sv_b, past, svn_ref[0])
    wb = swk_ref.shape[3]
    lane_w = lax.broadcasted_iota(jnp.int32, (LANES, LANES), 1)
    new_k = jnp.where(lane_w == 0, jnp.broadcast_to(wkn_ref[0], (LANES, LANES)), 0.0)
    new_v = jnp.where(lane_w == 0, jnp.broadcast_to(wvn_ref[0], (LANES, LANES)), 0.0)
    kw = jnp.concatenate([swk_ref[0, 0], new_k], axis=1).astype(BF16)
    vw = jnp.concatenate([swv_ref[0, 0], new_v], axis=1).astype(BF16)
    last = lax.broadcasted_iota(jnp.int32, (LANES, wb), 1) == wb - 1
    nwk_ref[0, 0] = jnp.where(last, jnp.broadcast_to(wkn_ref[0], (LANES, wb)), pltpu.roll(swk_ref[0, 0], wb - 1, 1))
    nwv_ref[0, 0] = jnp.where(last, jnp.broadcast_to(wvn_ref[0], (LANES, wb)), pltpu.roll(swv_ref[0, 0], wb - 1, 1))
    for cp in copies:
        cp.wait()

    for t in range(past // LANES):
        rows = slice(t * LANES, (t + 1) * LANES)
        ck_b[rows, :] = ckT_b[:, rows].T
        cv_b[rows, :] = cvT_b[:, rows].T
    n_cmp_rows = past // CMP_STRIDE
    n_slc = -(-(past + 1) // SLC_BLK)
    kc = _compress_rows(ck_b, wcmpk_ref, n_cmp_rows).astype(BF16)
    vc = _compress_rows(cv_b, wcmpv_ref, n_cmp_rows).astype(BF16)

    q16 = _pad_heads(qb_ref[0])
    qr16 = _pad_heads(qbr_ref[0])
    qpos = jnp.full((S_ROWS, 1), past, jnp.int32)
    cidx = lax.broadcasted_iota(jnp.int32, (S_ROWS, n_cmp_rows), 1)
    vis_c = (cidx * CMP_STRIDE + CMP_LEN - 1 <= qpos) & (cidx < n_cmp)
    p_c = _softmax_once(_dot_nt(q16, kc), vis_c)
    o_c = _dot(p_c.astype(BF16), vc)
    row = lax.broadcasted_iota(jnp.int32, p_c.shape, 0)
    g0 = jnp.sum(jnp.where(row < 4, p_c, 0.0), axis=0, keepdims=True)
    g1 = jnp.sum(jnp.where((row >= 4) & (row < 8), p_c, 0.0), axis=0, keepdims=True)
    p_sum = jnp.where(row < 4, jnp.broadcast_to(g0, p_c.shape), jnp.broadcast_to(g1, p_c.shape))
    keys = _block_keys(_dot_f32_exact_rhs(p_sum, ov_ref[...]), qpos, n_slc)
    sel_box = []
    _topk_select(lambda c: keys, lambda c, v: sel_box.append(v), 1, float(min(SLC_TOPN, n_slc)), S_ROWS,
                 keys.shape[1])
    sel_b = sel_box[0]

    def slc_mask(c):
        kpos = c * S_CH + lax.broadcasted_iota(jnp.int32, (S_ROWS, S_CH), 1)
        return jnp.where(kpos <= qpos, _expand_blocks(sel_b, c, S_CH), 0.0)

    o_s = _attn_online(qr16, lambda c: sk_b[c].astype(BF16), lambda c: sv_b[c].astype(BF16), slc_mask,
                       sk_b.shape[0])
    pos_w = past - wb + lax.broadcasted_iota(jnp.int32, (S_ROWS, wb + LANES), 1)
    dist = qpos - pos_w
    vis_w = (pos_w >= 0) & (dist >= 0) & (dist <= WINDOW)
    p_w = _softmax_once(_dot(qr16, kw), vis_w)
    o_w = _dot_nt(p_w.astype(BF16), vw)
    g = g_ref[0]
    o_ref[0] = g[:, 0:1] * o_c[0:8] + g[:, 1:2] * o_s[0:8] + g[:, 2:3] * o_w[0:8]


def _nsa_sample(page_table, qb, qbr, g, skn, svn, wkn, wvn, w_ck, w_cv, ov, s_wk, s_wv, c_ck, c_cv, c_sk, c_sv,
                *, layer, past):
    n = page_table.shape[0]
    n_pages = page_table.shape[1]
    page = c_ck.shape[-1]
    wb = s_wk.shape[3]
    nch = (past + LANES) // S_CH
    n_cmp = (past + 1 - CMP_LEN) // CMP_STRIDE + 1
    per_seq = lambda width, lanes: pl.BlockSpec((1, width, lanes), lambda b, pt: (b, 0, 0))
    fixed = lambda shape: pl.BlockSpec(shape, lambda b, pt: (0,) * len(shape))
    state = pl.BlockSpec((1, 1, LANES, wb), lambda b, pt: (layer, b, 0, 0))
    new_state = pl.BlockSpec((1, 1, LANES, wb), lambda b, pt: (0, b, 0, 0))
    anyspec = pl.BlockSpec(memory_space=pl.ANY)
    grid_spec = pltpu.PrefetchScalarGridSpec(
        num_scalar_prefetch=1,
        grid=(n,),
        in_specs=[per_seq(8, LANES), per_seq(8, LANES), per_seq(8, 3)] + [per_seq(LANES, 1)] * 4
        + [fixed((CMP_LEN, LANES)), fixed((CMP_LEN, LANES)), fixed(ov.shape), state, state]
        + [anyspec] * 4,
        out_specs=[per_seq(8, LANES), new_state, new_state],
        scratch_shapes=[pltpu.VMEM((LANES, past), F32)] * 2 + [pltpu.VMEM((past, LANES), F32)] * 2
        + [pltpu.VMEM((nch, LANES, S_CH), F32)] * 2 + [pltpu.SemaphoreType.DMA((4,))],
    )
    st_shape = jax.ShapeDtypeStruct((1, n, LANES, wb), F32)
    return pl.pallas_call(
        functools.partial(_nsa_sample_kernel, layer=layer, past=past, n_pages=n_pages, page=page, n_cmp=n_cmp),
        grid_spec=grid_spec,
        out_shape=[jax.ShapeDtypeStruct((n, 8, LANES), F32), st_shape, st_shape],
        compiler_params=_cparams(("arbitrary",)),
        name="nsa_sample",
    )(page_table, qb, qbr, g, skn, svn, wkn, wvn, w_ck, w_cv, ov, s_wk, s_wv, c_ck, c_cv, c_sk, c_sv)


def _merge_kernel(x_ref, oa_ref, ob_ref, gate_ref, wpa_ref, wpb_ref, wo_ref, o_ref):
    gate = gate_ref[...]
    m = (gate[:, :D_MODEL] * _dot(oa_ref[...], wpa_ref[...]) + gate[:, D_MODEL:] * _dot(ob_ref[...], wpb_ref[...]))
    o_ref[...] = x_ref[...] + _dot(m.astype(BF16), wo_ref[...])


def _merge(x, o_a, o_b, gate, w_pa, w_pb, w_o, *, tm):
    T = x.shape[0]
    row = lambda i: (i, 0)
    fixed = lambda i: (0, 0)
    return pl.pallas_call(
        _merge_kernel,
        grid=(T // tm,),
        in_specs=[pl.BlockSpec((tm, D_MODEL), row), pl.BlockSpec((tm, 512), row), pl.BlockSpec((tm, 512), row),
                  pl.BlockSpec((tm, 2 * D_MODEL), row), pl.BlockSpec((512, D_MODEL), fixed),
                  pl.BlockSpec((512, D_MODEL), fixed), pl.BlockSpec((D_MODEL, D_MODEL), fixed)],
        out_specs=pl.BlockSpec((tm, D_MODEL), row),
        out_shape=jax.ShapeDtypeStruct((T, D_MODEL), F32),
        compiler_params=_cparams(("arbitrary",)),
        name="merge",
    )(x, o_a, o_b, gate, w_pa, w_pb, w_o)


def _align_w_in(w):
    z = lambda n: jnp.zeros((w.shape[0], n), w.dtype)
    return jnp.concatenate([w[:, :1352], z(56), w[:, 1352:2632], w[:, 2632:2656], z(104), w[:, 2656:]], axis=1)


def _heads_from_rows(o):
    n = o.shape[0]
    return jnp.concatenate([o[:, :4, :HEAD_DIM].reshape(n, 256), o[:, 4:, HEAD_DIM:].reshape(n, 256)], axis=1)


def _positions_last(x):
    lead = x.shape[:-3]
    nd = len(lead)
    x = jnp.transpose(x, tuple(range(nd)) + (nd + 1, nd + 2, nd))
    return x.reshape(lead + (x.shape[-3] * x.shape[-2], x.shape[-1]))


def _positions_first(x, kv):
    lead = x.shape[:-2]
    nd = len(lead)
    x = x.reshape(lead + (kv, x.shape[-2] // kv, x.shape[-1]))
    return jnp.transpose(x, tuple(range(nd)) + (nd + 2, nd, nd + 1))


def kernel(x_prompt, x_sample, cache_dsa_k, cache_dsa_v, cache_dsa_idx_k, cache_nsa_cmp_k, cache_nsa_cmp_v, cache_nsa_slc_k, cache_nsa_slc_v, state_nsa_win_k, state_nsa_win_v, page_table, norm_ffn1, w_ffn1_up, w_ffn1_down, norm_mix, w_in, w_cmp_k, w_cmp_v, w_proj_a, w_proj_b, w_out, norm_ffn2, w_ffn2_up, w_ffn2_down, norm_final):
    n_p, s_len, _ = x_prompt.shape
    n_s, t_s, _ = x_sample.shape
    assert t_s == 1 and s_len % P_CH == 0
    depth, n_pool, page = cache_dsa_k.shape[:3]
    past = page_table.shape[1] * page
    wb = state_nsa_win_k.shape[2]
    assert (past + LANES) % S_CH == 0 and wb == WINDOW and page == LANES

    xp = x_prompt.reshape(n_p * s_len, D_MODEL)
    xs = x_sample.reshape(n_s, D_MODEL)
    tab_p = _rope_tables(jnp.arange(s_len))
    tab_s = _rope_tables(jnp.full((n_s,), past))
    grouped = lambda c: jnp.transpose(c, (0, 1, 3, 4, 2))
    c_k, c_v = grouped(cache_dsa_k), grouped(cache_dsa_v)
    c_ck, c_cv, c_sk, c_sv = (grouped(cache_nsa_cmp_k), grouped(cache_nsa_cmp_v), grouped(cache_nsa_slc_k),
                              grouped(cache_nsa_slc_v))
    c_ik = jnp.transpose(cache_dsa_idx_k, (0, 1, 3, 2))
    s_wk = _positions_last(state_nsa_win_k)
    s_wv = _positions_last(state_nsa_win_v)
    ov_p = _slc_overlap((s_len - CMP_LEN) // CMP_STRIDE + 1, s_len, s_len // CMP_STRIDE, LANES)
    ov_s = _slc_overlap((past + 1 - CMP_LEN) // CMP_STRIDE + 1, past + 1, past // CMP_STRIDE, 2 * LANES)
    row = lambda v: v.reshape(1, -1)
    bf = lambda w: w.astype(BF16)

    st_p, st_s = [], []
    for l in range(depth):
        w_ck = w_cmp_k[l].reshape(CMP_LEN, LANES)
        w_cv = w_cmp_v[l].reshape(CMP_LEN, LANES)
        w_al = bf(_align_w_in(w_in[l]))
        wu1, wd1, wu2, wd2 = bf(w_ffn1_up[l]), bf(w_ffn1_down[l]), bf(w_ffn2_up[l]), bf(w_ffn2_down[l])
        w_pa, w_pb, w_o = bf(w_proj_a[l]), bf(w_proj_b[l]), bf(w_out[l])
        g_fin = row(norm_final)
        last = l == depth - 1

        xp = _ffn(xp, row(norm_ffn1[l]), wu1, wd1, g_fin, tm=512, tf=D_FF // 2, final=False)
        pp = _proj(xp, row(norm_mix[l]), w_al, tab_p, tm=256, seq=s_len, channel_major=True)
        o_a = _dsa_prompt(pp["iq"], pp["qa"], pp["ikw"], pp["ikT"], pp["ka"], pp["va"], n=n_p, s=s_len)
        kc, vc = _compress_prompt(pp["ck"], pp["cv"], w_ck, w_cv, n=n_p, s=s_len)
        o_b = _nsa_prompt(pp["qb"], pp["qbr"], pp["gb"], kc, vc, ov_p, pp["sk"], pp["sv"], pp["wk"], pp["wv"],
                          n=n_p, s=s_len)
        xp = _merge(xp, o_a, o_b, pp["gate"], w_pa, w_pb, w_o, tm=512)
        xp = _ffn(xp, row(norm_ffn2[l]), wu2, wd2, g_fin, tm=512, tf=D_FF // 2, final=last)
        st_p.append((pp["ka"], pp["va"], pp["ikT"], pp["ckT"], pp["cvT"], pp["sk"], pp["sv"],
                     pp["wk"][:, :, s_len - wb:], pp["wv"][:, :, s_len - wb:]))

        xs = _ffn(xs, row(norm_ffn1[l]), wu1, wd1, g_fin, tm=n_s, tf=D_FF // 2, final=False)
        ps = _proj(xs, row(norm_mix[l]), w_al, tab_s, tm=n_s, seq=n_s, channel_major=False)
        heads = lambda q: q.astype(F32).reshape(n_s, 8, HEAD_DIM)
        dup = lambda q: jnp.tile(heads(q), (1, 1, 2))
        col = lambda t: t.reshape(n_s, LANES, 1)
        o_a = _dsa_sample(page_table, heads(ps["iq"]), ps["ikw"][:, IDX_DIM:IDX_DIM + IDX_HEADS].reshape(n_s, 8, 1),
                          dup(ps["qa"]), col(ps["ikw"]), col(ps["ka"]), col(ps["va"]), c_ik, c_k, c_v,
                          layer=l, past=past)
        o_b, nwk, nwv = _nsa_sample(page_table, dup(ps["qb"]), dup(ps["qbr"]),
                                    ps["gb"][:, :3 * B_HEADS].reshape(n_s, 8, 3),
                                    col(ps["sk"]), col(ps["sv"]), col(ps["wk"]), col(ps["wv"]), w_ck, w_cv, ov_s,
                                    s_wk, s_wv, c_ck, c_cv, c_sk, c_sv, layer=l, past=past)
        xs = _merge(xs, bf(_heads_from_rows(o_a)), bf(_heads_from_rows(o_b)), ps["gate"], w_pa, w_pb, w_o, tm=n_s)
        xs = _ffn(xs, row(norm_ffn2[l]), wu2, wd2, g_fin, tm=n_s, tf=D_FF // 2, final=last)
        kv1 = lambda t: t.reshape(n_s, 1, 2, HEAD_DIM)
        st_s.append((kv1(ps["ka"]), kv1(ps["va"]), ps["ikw"][:, :IDX_DIM].reshape(n_s, 1, IDX_DIM), kv1(ps["ck"]),
                     kv1(ps["cv"]), kv1(ps["sk"]), kv1(ps["sv"]), nwk[0], nwv[0]))

    outs = [xp.reshape(n_p, s_len, D_MODEL), xs.reshape(n_s, 1, D_MODEL)]
    for i in range(9):
        sp = jnp.stack([s[i] for s in st_p])
        ss = jnp.stack([s[i] for s in st_s])
        if i == 2:
            sp = jnp.transpose(sp, (0, 1, 3, 2))
        else:
            sp = _positions_first(sp, 2)
        if i >= 7:
            ss = _positions_first(ss, 2)
        outs += [sp, ss]
    return tuple(outs)
```

```python
import functools

import jax
import jax.numpy as jnp
import numpy as np
from jax import lax
from jax.experimental import pallas as pl
from jax.experimental.pallas import tpu as pltpu

D_MODEL = 1024
HEAD_DIM = 64
ROT_DIM = HEAD_DIM // 4
ROPE_THETA = 500000.0
A_KV = 2
IDX_HEADS = 8
IDX_DIM = 64
DSA_TOPK = 256
B_HEADS = 8
B_KV = 2
CMP_STRIDE = 16
CMP_LEN = 2 * CMP_STRIDE
SLC_BLK = 64
SLC_TOPN = 16
WINDOW = 512
D_FF = 2816
Q_BLOCK = 128
EPS = 1e-6
NEG = -1e30
INT_MIN = -(2 ** 31)

LANES = 128
VMEM_LIMIT = 56 * 1024 * 1024

F32 = jnp.float32
BF16 = jnp.bfloat16

C_QA, C_KA, C_VA, C_IQ, C_IKW, C_QB = 0, 512, 640, 768, 1280, 1408
C_CK, C_CV, C_SK, C_SV, C_WK, C_WV, C_GB, C_GATE = 1920, 2048, 2176, 2304, 2432, 2560, 2688, 2816
D_IN_ALIGNED = C_GATE + 2 * D_MODEL


def _cparams(sem):
    return pltpu.CompilerParams(dimension_semantics=sem, vmem_limit_bytes=VMEM_LIMIT)


def _rms(x, g):
    return x * lax.rsqrt(jnp.mean(x * x, axis=-1, keepdims=True) + EPS) * g


def _dot(a, b):
    return jnp.dot(a, b, preferred_element_type=F32)


def _dot_nt(a, b):
    return lax.dot_general(a, b, (((1,), (1,)), ((), ())), preferred_element_type=F32)


def _loop(n, body, init):
    if isinstance(n, int):
        carry = init
        for j in range(n):
            carry = body(j, carry)
        return carry
    return lax.fori_loop(0, n, body, init)


def _ffn_kernel(x_ref, g_ref, wa_ref, wb_ref, wd_ref, gf_ref, o_ref, h_s, acc_s, *, n_ff, final):
    j = pl.program_id(1)

    @pl.when(j == 0)
    def _():
        h_s[...] = _rms(x_ref[...], g_ref[...]).astype(BF16)
        acc_s[...] = jnp.zeros_like(acc_s)

    h = h_s[...]
    a = _dot(h, wa_ref[...])
    b = _dot(h, wb_ref[...])
    act = (a * jax.nn.sigmoid(a)) * b
    acc_s[...] += _dot(act.astype(BF16), wd_ref[...])

    @pl.when(j == n_ff - 1)
    def _():
        y = x_ref[...] + 0.5 * acc_s[...]
        if final:
            y = _rms(y, gf_ref[...])
        o_ref[...] = y


def _ffn(x, g, w_up, w_down, g_final, *, tm, tf, final):
    T = x.shape[0]
    n_ff = D_FF // tf
    return pl.pallas_call(
        functools.partial(_ffn_kernel, n_ff=n_ff, final=final),
        grid=(T // tm, n_ff),
        in_specs=[
            pl.BlockSpec((tm, D_MODEL), lambda i, j: (i, 0)),
            pl.BlockSpec((1, D_MODEL), lambda i, j: (0, 0)),
            pl.BlockSpec((D_MODEL, tf), lambda i, j: (0, j)),
            pl.BlockSpec((D_MODEL, tf), lambda i, j: (0, j + n_ff)),
            pl.BlockSpec((tf, D_MODEL), lambda i, j: (j, 0)),
            pl.BlockSpec((1, D_MODEL), lambda i, j: (0, 0)),
        ],
        out_specs=pl.BlockSpec((tm, D_MODEL), lambda i, j: (i, 0)),
        out_shape=jax.ShapeDtypeStruct((T, D_MODEL), F32),
        scratch_shapes=[pltpu.VMEM((tm, D_MODEL), BF16), pltpu.VMEM((tm, D_MODEL), F32)],
        compiler_params=_cparams(("arbitrary", "arbitrary")),
        name="ffn",
    )(x, g, w_up, w_up, w_down, g_final)


def _rope_tables(pos):
    half = ROT_DIM // 2
    inv = ROPE_THETA ** (-jnp.arange(half, dtype=F32) / half)
    ang = pos.astype(F32)[:, None] * inv
    cos, sin = jnp.cos(ang), jnp.sin(ang)
    n = pos.shape[0]
    one, zero = jnp.ones((n, HEAD_DIM - ROT_DIM), F32), jnp.zeros((n, HEAD_DIM - ROT_DIM), F32)
    z8 = jnp.zeros((n, half), F32)
    c = jnp.concatenate([cos, cos, one], axis=1)
    slo = jnp.concatenate([-sin, z8, zero], axis=1)
    shi = jnp.concatenate([z8, sin, zero], axis=1)
    return tuple(jnp.concatenate([t, t], axis=1) for t in (c, slo, shi))


_PROJ_SAMPLE_OUTS = (
    [("tok", n, 512, BF16) for n in ("qa", "iq", "qb", "qbr")]
    + [("tok", n, LANES, F32) for n in ("ikw", "ka", "va", "ck", "cv", "sk", "sv", "wk", "wv", "gb")]
    + [("tok", "gate", 2 * D_MODEL, F32)])
_PROJ_PROMPT_OUTS = (
    [("tok", n, LANES, F32) for n in ("ikw", "ka", "sk", "wk", "ck", "cv")]
    + [("tok", "gate", 2 * D_MODEL, F32)]
    + [("chan", n, 512, BF16) for n in ("qaT", "iqT", "qbT", "qbrT")]
    + [("chan", "iwT", IDX_HEADS, F32), ("chan", "gbT", 3 * B_HEADS, F32), ("chan", "ikT", IDX_DIM, F32)]
    + [("chan", n, LANES, F32) for n in ("kaT", "vaT", "ckT", "cvT", "skT", "svT", "wkT", "wvT")])


def _proj_kernel(x_ref, g_ref, w_ref, c_ref, slo_ref, shi_ref, *out_refs, outs):
    o = {name: ref for (_, name, _, _), ref in zip(outs, out_refs)}
    h = _rms(x_ref[...], g_ref[...]).astype(BF16)
    C, SLO, SHI = c_ref[...], slo_ref[...], shi_ref[...]
    scale = HEAD_DIM ** -0.5

    def mm(c0, width):
        return _dot(h, w_ref[:, c0:c0 + width])

    def rope(p):
        outs = []
        for j in range(p.shape[1] // LANES):
            xj = p[:, j * LANES:(j + 1) * LANES]
            outs.append(xj * C + pltpu.roll(xj, LANES - ROT_DIM // 2, 1) * SLO
                        + pltpu.roll(xj, ROT_DIM // 2, 1) * SHI)
        return outs[0] if len(outs) == 1 else jnp.concatenate(outs, axis=1)

    def put(name, v):
        if name in o:
            o[name][...] = v.astype(o[name].dtype)
        if name + "T" in o:
            ref = o[name + "T"]
            ref[0] = v.T[0:ref.shape[1], :].astype(ref.dtype)

    put("qa", rope(mm(C_QA, 512)) * scale)
    put("iq", rope(mm(C_IQ, 512)) * (IDX_DIM ** -0.5))
    qb = mm(C_QB, 512)
    put("qb", qb * scale)
    put("qbr", rope(qb) * scale)
    ikw = mm(C_IKW, 128)
    lane = lax.broadcasted_iota(jnp.int32, ikw.shape, 1)
    ikw = jnp.where(lane < IDX_DIM, rope(ikw), ikw)
    put("ikw", ikw)
    if "ikT" in o:
        ikw_t = ikw.T
        o["ikT"][0] = ikw_t[0:IDX_DIM, :]
        o["iwT"][0] = ikw_t[IDX_DIM:IDX_DIM + IDX_HEADS, :]
    put("gb", jax.nn.sigmoid(mm(C_GB, 128)))
    put("gate", jax.nn.sigmoid(mm(C_GATE, 2 * D_MODEL)))
    put("ck", mm(C_CK, 128))
    put("cv", mm(C_CV, 128))
    put("ka", rope(mm(C_KA, 128)))
    put("va", mm(C_VA, 128))
    put("sk", rope(mm(C_SK, 128)))
    put("sv", mm(C_SV, 128))
    put("wk", rope(mm(C_WK, 128)))
    put("wv", mm(C_WV, 128))


def _proj(x, g, w_al, tables, outs, *, tm, seq):
    T = x.shape[0]
    per_seq = seq // tm
    row = lambda i: (i, 0)
    fixed = lambda i: (0, 0)
    tab = pl.BlockSpec((tm, LANES), lambda i: (i % per_seq, 0))
    shapes, specs = [], []
    for kind, _, ch, dt in outs:
        if kind == "tok":
            shapes.append(jax.ShapeDtypeStruct((T, ch), dt))
            specs.append(pl.BlockSpec((tm, ch), row))
        else:
            shapes.append(jax.ShapeDtypeStruct((T // seq, ch, seq), dt))
            specs.append(pl.BlockSpec((1, ch, tm), lambda i: (i // per_seq, 0, i % per_seq)))
    res = pl.pallas_call(
        functools.partial(_proj_kernel, outs=tuple(outs)),
        grid=(T // tm,),
        in_specs=[pl.BlockSpec((tm, D_MODEL), row), pl.BlockSpec((1, D_MODEL), fixed),
                  pl.BlockSpec((D_MODEL, D_IN_ALIGNED), fixed), tab, tab, tab],
        out_specs=specs,
        out_shape=shapes,
        compiler_params=_cparams(("arbitrary",)),
        name="proj",
    )(x, g, w_al, *tables)
    return {name: r for (_, name, _, _), r in zip(outs, res)}


def _sort_key(score):
    bits = lax.bitcast_convert_type(score, jnp.int32)
    key = jnp.where(bits < 0, bits ^ jnp.int32(0x7FFFFFFF), bits)
    return jnp.where(score == 0.0, jnp.int32(0), key)


def _fold_lanes(x):
    acc = x[:, :LANES]
    for j in range(1, x.shape[1] // LANES):
        acc = acc + x[:, j * LANES:(j + 1) * LANES]
    return acc


def _count(get_keys, nch, rows, pred):
    def body(c, acc):
        return acc + _fold_lanes(jnp.where(pred(get_keys(c)), 1.0, 0.0))
    acc = _loop(nch, body, jnp.zeros((rows, LANES), F32))
    return jnp.sum(acc, axis=-1, keepdims=True)


def _kth_largest(get_keys, nch, k, rows):
    def bit_body(b, t):
        trial = t + lax.shift_left(jnp.int32(1), jnp.int32(31) - b)
        cnt = _count(get_keys, nch, rows, lambda kc: kc >= trial)
        return jnp.where(cnt >= k, trial, t)
    return lax.fori_loop(0, 32, bit_body, jnp.full((rows, 1), INT_MIN, jnp.int32))


def _prefix_matrix():
    r = lax.broadcasted_iota(jnp.int32, (LANES, LANES), 0)
    c = lax.broadcasted_iota(jnp.int32, (LANES, LANES), 1)
    return jnp.where(r <= c, 1.0, 0.0).astype(BF16)


def _topk_select(get_keys, put_sel, nch, k, rows, ch):
    t = _kth_largest(get_keys, nch, k, rows)
    need = k - _count(get_keys, nch, rows, lambda kc: kc > t)
    tri = _prefix_matrix()

    def body(c, run):
        kc = get_keys(c)
        sels = []
        for j in range(ch // LANES):
            kj = kc[:, j * LANES:(j + 1) * LANES]
            eq = kj == t
            pj = _dot(jnp.where(eq, 1.0, 0.0).astype(BF16), tri)
            take = (kj > t) | (eq & ((pj + run) <= need))
            sels.append(jnp.where(take & (kj != INT_MIN), 1.0, 0.0))
            run = run + pj[:, LANES - 1:LANES]
        put_sel(c, sels[0] if len(sels) == 1 else jnp.concatenate(sels, axis=1))
        return run

    _loop(nch, body, jnp.zeros((rows, 1), F32))


def _attn_online(q, get_k, get_v, get_mask, nch):
    m_rows = q.shape[0]

    def body(c, carry):
        m, l, acc = carry
        s = _dot(q, get_k(c))
        keep = jnp.broadcast_to(get_mask(c), s.shape) > 0.5
        s = jnp.where(keep, s, NEG)
        m_new = jnp.maximum(m, jnp.max(s, axis=-1, keepdims=True))
        alpha = jnp.exp(m - m_new)
        p = jnp.where(keep, jnp.exp(s - m_new), 0.0)
        l = alpha * l + jnp.sum(p, axis=-1, keepdims=True)
        acc = alpha * acc + _dot_nt(p.astype(BF16), get_v(c))
        return m_new, l, acc

    init = (jnp.full((m_rows, 1), NEG, F32), jnp.zeros((m_rows, 1), F32), jnp.zeros((m_rows, LANES), F32))
    _, l, acc = _loop(nch, body, init)
    return acc / jnp.where(l > 0.0, l, 1.0)


def _softmax_once(s, keep):
    s = jnp.where(keep, s, NEG)
    e = jnp.where(keep, jnp.exp(s - jnp.max(s, axis=-1, keepdims=True)), 0.0)
    l = jnp.sum(e, axis=-1, keepdims=True)
    return e / jnp.where(l > 0.0, l, 1.0)


def _dot_f32_exact_rhs(p, w_bf16):
    p1 = p.astype(BF16)
    r1 = p - p1.astype(F32)
    p2 = r1.astype(BF16)
    p3 = (r1 - p2.astype(F32)).astype(BF16)
    return _dot(p1, w_bf16) + _dot(p2, w_bf16) + _dot(p3, w_bf16)


def _head_queries(qT):
    zeros = jnp.zeros((HEAD_DIM, qT.shape[1]), qT.dtype)
    tiles = []
    for h in range(8):
        x = qT[h * HEAD_DIM:(h + 1) * HEAD_DIM, :]
        tiles.append(jnp.concatenate([x, zeros] if h < 4 else [zeros, x], axis=0))
    return jnp.concatenate(tiles, axis=1).astype(BF16)


def _per_head(x, heads, fn):
    r_w = x.shape[1] // heads
    return jnp.concatenate([fn(r, x[:, r * r_w:(r + 1) * r_w]) for r in range(heads)], axis=1)


def _attn_keys_on_rows(qT, get_k, get_vT, get_bias, nch, heads):
    width = qT.shape[1]

    def body(c, carry):
        m, l, acc = carry
        biases = get_bias(c)
        s = _per_head(_dot(get_k(c), qT), heads, lambda r, t: t + biases[r * len(biases) // heads])
        m_new = jnp.maximum(m, jnp.max(s, axis=0, keepdims=True))
        alpha = jnp.exp(m - m_new)
        p = jnp.exp(s - m_new)
        l = alpha * l + jnp.sum(p, axis=0, keepdims=True)
        acc = alpha * acc + _dot(get_vT(c), p.astype(BF16))
        return m_new, l, acc

    init = (jnp.full((1, width), NEG, F32), jnp.zeros((1, width), F32), jnp.zeros((LANES, width), F32))
    _, l, acc = _loop(nch, body, init)
    return acc / l


def _softmax_keys_on_rows(s, keep, heads):
    def one(r, t):
        t = jnp.where(keep, t, NEG)
        e = jnp.where(keep, jnp.exp(t - jnp.max(t, axis=0, keepdims=True)), 0.0)
        l = jnp.sum(e, axis=0, keepdims=True)
        return e / jnp.where(l > 0.0, l, 1.0)
    return _per_head(s, heads, one)


def _heads_to_tokens(o):
    r_w = o.shape[1] // 8
    tiles = []
    for h in range(8):
        g = h // 4
        tiles.append(o[g * HEAD_DIM:(g + 1) * HEAD_DIM, h * r_w:(h + 1) * r_w])
    return jnp.concatenate(tiles, axis=0).T


def _lower_triangle(n):
    r = lax.broadcasted_iota(jnp.int32, (n, n), 0)
    c = lax.broadcasted_iota(jnp.int32, (n, n), 1)
    return jnp.where(c <= r, 1.0, 0.0).astype(BF16)


def _dot_f32_exact_lhs(w_bf16, p):
    p1 = p.astype(BF16)
    r1 = p - p1.astype(F32)
    p2 = r1.astype(BF16)
    p3 = (r1 - p2.astype(F32)).astype(BF16)
    return _dot(w_bf16, p1) + _dot(w_bf16, p2) + _dot(w_bf16, p3)


P_CH = 512
I16_MIN = -(2 ** 15)


def _count16(ref, nch, pred):
    rows = 16

    def body(c, acc):
        ind = jnp.where(pred(ref[c]), jnp.int16(1), jnp.int16(0))
        parts = [ind[j * rows:(j + 1) * rows] for j in range(ind.shape[0] // rows)]
        while len(parts) > 1:
            parts = [a + b for a, b in zip(parts[0::2], parts[1::2])]
        return acc + parts[0]

    acc = lax.fori_loop(0, nch, body, jnp.zeros((rows, ref.shape[2]), jnp.int16))
    return jnp.sum(acc.astype(F32), axis=0, keepdims=True)


def _kth_largest16(ref, nch, k):
    def bit_body(b, t):
        trial = t + lax.shift_left(jnp.int32(1), jnp.int32(15) - b)
        trial16 = trial.astype(jnp.int16)
        cnt = _count16(ref, nch, lambda x: x >= trial16)
        return jnp.where(cnt >= k, trial, t)
    return lax.fori_loop(0, 16, bit_body, jnp.full((1, ref.shape[2]), I16_MIN, jnp.int32))


def _topk_bias_keys_on_rows(keys_ref, hi_ref, lo_ref, bias_ref, nch, k, ch):
    h = _kth_largest16(hi_ref, nch, k)
    h16 = h.astype(jnp.int16)
    k_lo = k - _count16(hi_ref, nch, lambda x: x > h16)

    def narrow(c, carry):
        lo_ref[c] = jnp.where(hi_ref[c] == h16, lo_ref[c], jnp.int16(I16_MIN))
        return carry

    lax.fori_loop(0, nch, narrow, 0)
    t_lo = _kth_largest16(lo_ref, nch, k_lo)
    t_lo16 = t_lo.astype(jnp.int16)
    t = lax.shift_left(h, 16) + (t_lo + 32768)
    n_above = k - k_lo + _count16(lo_ref, nch, lambda x: x > t_lo16)
    n_tied = _count16(lo_ref, nch, lambda x: x == t_lo16)
    need = k - n_above
    has_split_tie = jnp.max(n_tied - need) > 0.0

    @pl.when(jnp.logical_not(has_split_tie))
    def _():
        def body(c, carry):
            kc = keys_ref[c]
            bias_ref[c] = jnp.where((kc >= t) & (kc != INT_MIN), 0.0, NEG)
            return carry
        lax.fori_loop(0, nch, body, 0)

    @pl.when(has_split_tie)
    def _():
        tri = _lower_triangle(ch)

        def body(c, run):
            kc = keys_ref[c]
            eq = kc == t
            pref = _dot(tri, jnp.where(eq, 1.0, 0.0).astype(BF16)) + run
            take = (kc > t) | (eq & (pref <= need))
            bias_ref[c] = jnp.where(take & (kc != INT_MIN), 0.0, NEG)
            return pref[ch - 1:ch, :]
        lax.fori_loop(0, nch, body, jnp.zeros((1, keys_ref.shape[2]), F32))


def _dsa_prompt_kernel(iqT_ref, qaT_ref, iwT_ref, ikw_ref, ka_ref, vT_ref, o_ref,
                       ik_s, k_s, v_s, keys_s, hi_s, lo_s, bias_s, *, k_sel, n_chunks):
    i = pl.program_id(1)
    R = Q_BLOCK

    @pl.when(i == 0)
    def _():
        ik_s[...] = ikw_ref[:, 0:IDX_DIM].astype(BF16)
        k_s[...] = ka_ref[...].astype(BF16)
        for c in range(n_chunks):
            v_s[c] = vT_ref[0, :, c * P_CH:(c + 1) * P_CH].astype(BF16)

    nch = lax.shift_right_logical(i, 2) + 1
    qpos = i * R + lax.broadcasted_iota(jnp.int32, (1, R), 1)
    key_rows = lambda ref, c: ref[pl.ds(pl.multiple_of(c * P_CH, P_CH), P_CH), :]

    iq_t = iqT_ref[0]
    iq_heads = jnp.concatenate([iq_t[h * IDX_DIM:(h + 1) * IDX_DIM, :] for h in range(IDX_HEADS)], axis=1)
    w_t = iwT_ref[0] * (IDX_HEADS ** -0.5)

    def score_body(c, carry):
        d = _dot(key_rows(ik_s, c), iq_heads)
        tot = jnp.zeros((P_CH, R), F32)
        for h in range(IDX_HEADS):
            tot = tot + w_t[h:h + 1, :] * jnp.maximum(d[:, h * R:(h + 1) * R], 0.0)
        kpos = c * P_CH + lax.broadcasted_iota(jnp.int32, (P_CH, R), 0)
        key = jnp.where(kpos <= qpos, _sort_key(tot), INT_MIN)
        keys_s[c] = key
        hi_s[c] = lax.shift_right_arithmetic(key, 16).astype(jnp.int16)
        lo_s[c] = ((key & 0xFFFF) - 32768).astype(jnp.int16)
        return carry

    lax.fori_loop(0, nch, score_body, 0)
    _topk_bias_keys_on_rows(keys_s, hi_s, lo_s, bias_s, nch, float(k_sel), P_CH)

    o = _attn_keys_on_rows(_head_queries(qaT_ref[0]), lambda c: key_rows(k_s, c), lambda c: v_s[c],
                           lambda c: [bias_s[c]], nch, 8)
    o_ref[...] = _heads_to_tokens(o).astype(o_ref.dtype)


def _dsa_prompt(iqT, qaT, iwT, ikw, ka, vaT, *, n, s):
    nq = s // Q_BLOCK
    n_chunks = s // P_CH
    k_sel = min(DSA_TOPK, s // 4)
    qrow = lambda b, i: (b * nq + i, 0)
    qcol = lambda b, i: (b, 0, i)
    tokens = lambda b, i: (b, 0)
    full = lambda b, i: (b, 0, 0)
    return pl.pallas_call(
        functools.partial(_dsa_prompt_kernel, k_sel=k_sel, n_chunks=n_chunks),
        grid=(n, nq),
        in_specs=[pl.BlockSpec((1, 512, Q_BLOCK), qcol), pl.BlockSpec((1, 512, Q_BLOCK), qcol),
                  pl.BlockSpec((1, IDX_HEADS, Q_BLOCK), qcol),
                  pl.BlockSpec((s, LANES), tokens), pl.BlockSpec((s, LANES), tokens),
                  pl.BlockSpec((1, LANES, s), full)],
        out_specs=pl.BlockSpec((Q_BLOCK, 512), qrow),
        out_shape=jax.ShapeDtypeStruct((n * s, 512), BF16),
        scratch_shapes=[pltpu.VMEM((s, IDX_DIM), BF16), pltpu.VMEM((s, LANES), BF16),
                        pltpu.VMEM((n_chunks, LANES, P_CH), BF16),
                        pltpu.VMEM((n_chunks, P_CH, Q_BLOCK), jnp.int32),
                        pltpu.VMEM((n_chunks, P_CH, Q_BLOCK), jnp.int16),
                        pltpu.VMEM((n_chunks, P_CH, Q_BLOCK), jnp.int16),
                        pltpu.VMEM((n_chunks, P_CH, Q_BLOCK), F32)],
        compiler_params=_cparams(("arbitrary", "arbitrary")),
        name="dsa_prompt",
    )(iqT, qaT, iwT, ikw, ka, vaT)


def _compress_rows(src_ref, w_ref, n_chunks):
    lo = jnp.zeros((n_chunks, LANES), F32)
    hi = jnp.zeros((n_chunks, LANES), F32)
    for i in range(CMP_STRIDE):
        xi = src_ref[pl.ds(i, n_chunks, stride=CMP_STRIDE), :]
        lo = lo + xi * w_ref[i:i + 1, :]
        hi = hi + xi * w_ref[CMP_STRIDE + i:CMP_STRIDE + i + 1, :]
    return lo + pltpu.roll(hi, n_chunks - 1, 0)


def _compress_kernel(ck_ref, cv_ref, wk_ref, wv_ref, kc_o, vc_o, *, n_chunks):
    kc_o[...] = _compress_rows(ck_ref, wk_ref, n_chunks)
    vc_o[...] = _compress_rows(cv_ref, wv_ref, n_chunks)


def _compress_prompt(ck, cv, w_ck, w_cv, *, n, s):
    n_chunks = s // CMP_STRIDE
    full = lambda b: (b, 0)
    fixed = lambda b: (0, 0)
    shape = jax.ShapeDtypeStruct((n * n_chunks, LANES), F32)
    return pl.pallas_call(
        functools.partial(_compress_kernel, n_chunks=n_chunks),
        grid=(n,),
        in_specs=[pl.BlockSpec((s, LANES), full), pl.BlockSpec((s, LANES), full),
                  pl.BlockSpec((CMP_LEN, LANES), fixed), pl.BlockSpec((CMP_LEN, LANES), fixed)],
        out_specs=[pl.BlockSpec((n_chunks, LANES), full)] * 2,
        out_shape=[shape, shape],
        compiler_params=_cparams(("arbitrary",)),
        name="compress_prompt",
    )(ck, cv, w_ck, w_cv)


def _slc_overlap(n_cmp, n_keys, rows, cols):
    n_slc = -(-n_keys // SLC_BLK)
    j = np.arange(n_cmp)[:, None]
    sb = np.arange(n_slc)[None, :]
    lo = np.maximum(j * CMP_STRIDE, sb * SLC_BLK)
    hi = np.minimum(j * CMP_STRIDE + CMP_LEN, (sb + 1) * SLC_BLK)
    ov = np.zeros((rows, cols), np.float32)
    ov[:n_cmp, :n_slc] = np.maximum(hi - lo, 0) / CMP_STRIDE
    return jnp.asarray(ov, dtype=BF16)


def _block_keys(p_s, qpos, n_slc):
    blk = lax.broadcasted_iota(jnp.int32, p_s.shape, 1)
    cur = lax.shift_right_logical(qpos, 6)
    forced = (blk == 0) | (blk == cur) | (blk == cur - 1)
    p_s = jnp.where(forced, 1e6, p_s)
    p_s = jnp.where(blk * SLC_BLK <= qpos, p_s, NEG)
    return jnp.where(blk < n_slc, _sort_key(p_s), INT_MIN)


def _expand_blocks(sel_b, c, ch):
    nb = sel_b.shape[1]
    blk = lax.broadcasted_iota(jnp.int32, (nb, ch), 0)
    kblk = lax.shift_right_logical(c * ch + lax.broadcasted_iota(jnp.int32, (nb, ch), 1), 6)
    e = jnp.where(blk == kblk, 1.0, 0.0).astype(BF16)
    return _dot(sel_b.astype(BF16), e)


def _block_keys_on_rows(p_s, qpos, n_slc):
    blk = lax.broadcasted_iota(jnp.int32, p_s.shape, 0)
    cur = lax.shift_right_logical(qpos, 6)
    forced = (blk == 0) | (blk == cur) | (blk == cur - 1)
    p_s = jnp.where(forced, 1e6, p_s)
    p_s = jnp.where(blk * SLC_BLK <= qpos, p_s, NEG)
    return jnp.where(blk < n_slc, _sort_key(p_s), INT_MIN)


def _topk_mask_keys_on_rows(keys, k):
    count = lambda m: jnp.sum(jnp.where(m, 1.0, 0.0), axis=0, keepdims=True)

    def bit_body(b, t):
        trial = t + lax.shift_left(jnp.int32(1), jnp.int32(31) - b)
        return jnp.where(count(keys >= trial) >= k, trial, t)

    t = lax.fori_loop(0, 32, bit_body, jnp.full((1, keys.shape[1]), INT_MIN, jnp.int32))
    need = k - count(keys > t)
    eq = keys == t
    pref = _dot(_lower_triangle(keys.shape[0]), jnp.where(eq, 1.0, 0.0).astype(BF16))
    take = (keys > t) | (eq & (pref <= need))
    return jnp.where(take & (keys != INT_MIN), 1.0, 0.0)


def _nsa_prompt_kernel(qbT_ref, qbrT_ref, gbT_ref, kc_ref, vc_ref, ovT_ref, sk_ref, svT_ref, wk_ref, wvT_ref, o_ref,
                       kc_s, vc_s, sk_s, sv_s, wk_s, wv_s, *, s_len, n_cmp):
    i = pl.program_id(1)
    R = Q_BLOCK
    n_cmp_rows = s_len // CMP_STRIDE
    n_slc = s_len // SLC_BLK
    w_pad = WINDOW // R
    w_tiles = WINDOW // R + 1

    @pl.when(i == 0)
    def _():
        kc_s[...] = kc_ref[...].astype(BF16)
        for t in range(n_cmp_rows // LANES):
            vc_s[:, t * LANES:(t + 1) * LANES] = vc_ref[t * LANES:(t + 1) * LANES, :].T.astype(BF16)
        sk_s[...] = sk_ref[...].astype(BF16)
        for c in range(s_len // P_CH):
            sv_s[c] = svT_ref[0, :, c * P_CH:(c + 1) * P_CH].astype(BF16)
        wk_s[0:WINDOW, :] = jnp.zeros((WINDOW, LANES), BF16)
        wk_s[WINDOW:WINDOW + s_len, :] = wk_ref[...].astype(BF16)
        for c in range(w_pad):
            wv_s[c] = jnp.zeros((LANES, R), BF16)
        for c in range(s_len // R):
            wv_s[w_pad + c] = wvT_ref[0, :, c * R:(c + 1) * R].astype(BF16)

    nch = lax.shift_right_logical(i, 2) + 1
    qpos = i * R + lax.broadcasted_iota(jnp.int32, (1, R), 1)
    key_rows = lambda ref, c: ref[pl.ds(pl.multiple_of(c * P_CH, P_CH), P_CH), :]
    q, qr, gates = _head_queries(qbT_ref[0]), _head_queries(qbrT_ref[0]), gbT_ref[0]

    cidx = lax.broadcasted_iota(jnp.int32, (n_cmp_rows, R), 0)
    vis_c = (cidx * CMP_STRIDE + CMP_LEN - 1 <= qpos) & (cidx < n_cmp)
    pos_w = i * R - WINDOW + lax.broadcasted_iota(jnp.int32, (WINDOW + R, R), 0)
    dist = qpos - pos_w
    vis_w = (pos_w >= 0) & (dist >= 0) & (dist <= WINDOW)
    kw = wk_s[pl.ds(pl.multiple_of(i * R, R), WINDOW + R), :]
    vw = jnp.concatenate([wv_s[i + t] for t in range(w_tiles)], axis=1)

    p_c = _softmax_keys_on_rows(_dot(kc_s[...], q), vis_c, 8)
    o_c = _dot(vc_s[...], p_c.astype(BF16))
    keys = []
    for g in range(B_KV):
        p_sum = p_c[:, 4 * g * R:(4 * g + 1) * R]
        for h in range(4 * g + 1, 4 * g + 4):
            p_sum = p_sum + p_c[:, h * R:(h + 1) * R]
        keys.append(_block_keys_on_rows(_dot_f32_exact_lhs(ovT_ref[...], p_sum), qpos, n_slc))
    sel = _topk_mask_keys_on_rows(jnp.concatenate(keys, axis=1), float(min(SLC_TOPN, n_slc))).astype(BF16)

    def slc_bias(c):
        n_blk = sel.shape[0]
        key_blk = lax.shift_right_logical(c * P_CH + lax.broadcasted_iota(jnp.int32, (P_CH, n_blk), 0), 6)
        member = jnp.where(key_blk == lax.broadcasted_iota(jnp.int32, (P_CH, n_blk), 1), 1.0, 0.0).astype(BF16)
        kpos = c * P_CH + lax.broadcasted_iota(jnp.int32, (P_CH, R), 0)
        chosen = _dot(member, sel) > 0.5
        return [jnp.where(chosen[:, g * R:(g + 1) * R] & (kpos <= qpos), 0.0, NEG) for g in range(B_KV)]

    o_s = _attn_keys_on_rows(qr, lambda c: key_rows(sk_s, c), lambda c: sv_s[c], slc_bias, nch, 8)
    p_w = _softmax_keys_on_rows(_dot(kw, qr), vis_w, 8)
    o_w = _dot(vw, p_w.astype(BF16))

    def mix(h, o_s_h):
        cols = slice(h * R, (h + 1) * R)
        return (gates[3 * h:3 * h + 1, :] * o_c[:, cols] + gates[3 * h + 1:3 * h + 2, :] * o_s_h
                + gates[3 * h + 2:3 * h + 3, :] * o_w[:, cols])

    o_ref[...] = _heads_to_tokens(_per_head(o_s, 8, mix)).astype(o_ref.dtype)


def _nsa_prompt(qbT, qbrT, gbT, kc, vc, ovT, sk, svT, wk, wvT, *, n, s):
    nq = s // Q_BLOCK
    n_cmp_rows = s // CMP_STRIDE
    n_cmp = (s - CMP_LEN) // CMP_STRIDE + 1
    qrow = lambda b, i: (b * nq + i, 0)
    qcol = lambda b, i: (b, 0, i)
    per_b = lambda b, i: (b, 0)
    full = lambda b, i: (b, 0, 0)
    fixed = lambda b, i: (0, 0)
    return pl.pallas_call(
        functools.partial(_nsa_prompt_kernel, s_len=s, n_cmp=n_cmp),
        grid=(n, nq),
        in_specs=[pl.BlockSpec((1, 512, Q_BLOCK), qcol), pl.BlockSpec((1, 512, Q_BLOCK), qcol),
                  pl.BlockSpec((1, 3 * B_HEADS, Q_BLOCK), qcol),
                  pl.BlockSpec((n_cmp_rows, LANES), per_b), pl.BlockSpec((n_cmp_rows, LANES), per_b),
                  pl.BlockSpec(ovT.shape, fixed),
                  pl.BlockSpec((s, LANES), per_b), pl.BlockSpec((1, LANES, s), full),
                  pl.BlockSpec((s, LANES), per_b), pl.BlockSpec((1, LANES, s), full)],
        out_specs=pl.BlockSpec((Q_BLOCK, 512), qrow),
        out_shape=jax.ShapeDtypeStruct((n * s, 512), BF16),
        scratch_shapes=[pltpu.VMEM((n_cmp_rows, LANES), BF16), pltpu.VMEM((LANES, n_cmp_rows), BF16),
                        pltpu.VMEM((s, LANES), BF16), pltpu.VMEM((s // P_CH, LANES, P_CH), BF16),
                        pltpu.VMEM((WINDOW + s, LANES), BF16),
                        pltpu.VMEM(((WINDOW + s) // Q_BLOCK, LANES, Q_BLOCK), BF16)],
        compiler_params=_cparams(("arbitrary", "arbitrary")),
        name="nsa_prompt",
    )(qbT, qbrT, gbT, kc, vc, ovT, sk, svT, wk, wvT)


S_CH = 640
S_ROWS = 16


def _page_dst(buf, p, page):
    c, off = divmod(p * page, S_CH)
    return c, off


def _start_all(copies):
    for cp in copies:
        cp.start()
    return copies


def _page_copies_chunked(pt_ref, b, layer, pool, buf, sem, n_pages, page, grouped):
    copies = []
    for p in range(n_pages):
        pg = pt_ref[b, p]
        c, off = _page_dst(buf, p, page)
        if grouped:
            for g in range(2):
                copies.append(pltpu.make_async_copy(
                    pool.at[layer, pg, g], buf.at[c, pl.ds(g * HEAD_DIM, HEAD_DIM), pl.ds(off, page)], sem))
        else:
            copies.append(pltpu.make_async_copy(pool.at[layer, pg], buf.at[c, :, pl.ds(off, page)], sem))
    return copies


def _page_copies_flat(pt_ref, b, layer, pool, buf, sem, n_pages, page):
    copies = []
    for p in range(n_pages):
        pg = pt_ref[b, p]
        for g in range(2):
            copies.append(pltpu.make_async_copy(
                pool.at[layer, pg, g], buf.at[pl.ds(g * HEAD_DIM, HEAD_DIM), pl.ds(p * page, page)], sem))
    return copies


def _set_new_column(buf, past, col):
    c, off = divmod(past, S_CH)
    ch = buf.shape[1]
    lane = lax.broadcasted_iota(jnp.int32, (ch, LANES), 1)
    buf[c, :, off:off + LANES] = jnp.where(lane == 0, jnp.broadcast_to(col, (ch, LANES)), 0.0)


def _pad_heads(q8):
    row = lax.broadcasted_iota(jnp.int32, q8.shape, 0)
    lane = lax.broadcasted_iota(jnp.int32, q8.shape, 1)
    q8 = jnp.where((row // 4) == (lane // HEAD_DIM), q8, 0.0)
    return jnp.concatenate([q8, jnp.zeros_like(q8)], axis=0).astype(BF16)


def _dsa_sample_kernel(pt_ref, iq_ref, iw_ref, qa_ref, ikn_ref, kan_ref, van_ref, cik_hbm, ck_hbm, cv_hbm,
                       o_ref, ik_b, k_b, v_b, keys_s, sel_s, sems, *, layer, past, n_pages, page, k_sel):
    b = pl.program_id(0)
    copies = _start_all(
        _page_copies_chunked(pt_ref, b, layer, cik_hbm, ik_b, sems.at[0], n_pages, page, False)
        + _page_copies_chunked(pt_ref, b, layer, ck_hbm, k_b, sems.at[1], n_pages, page, True)
        + _page_copies_chunked(pt_ref, b, layer, cv_hbm, v_b, sems.at[2], n_pages, page, True))
    _set_new_column(ik_b, past, ikn_ref[0][0:IDX_DIM, :])
    _set_new_column(k_b, past, kan_ref[0])
    _set_new_column(v_b, past, van_ref[0])
    for cp in copies:
        cp.wait()

    nch = ik_b.shape[0]
    iq16 = jnp.concatenate([iq_ref[0], jnp.zeros((8, IDX_DIM), F32)], axis=0).astype(BF16)
    w16 = jnp.concatenate([iw_ref[0] * (IDX_HEADS ** -0.5), jnp.zeros((8, 1), F32)], axis=0)

    def score_body(c, carry):
        d = jnp.maximum(_dot(iq16, ik_b[c].astype(BF16)), 0.0)
        tot = jnp.sum(w16 * d, axis=0, keepdims=True)
        kpos = c * S_CH + lax.broadcasted_iota(jnp.int32, (1, S_CH), 1)
        key = jnp.where(kpos <= past, _sort_key(tot), INT_MIN)
        keys_s[c] = jnp.broadcast_to(key, (8, S_CH))
        return carry

    lax.fori_loop(0, nch, score_body, 0)

    def put_sel(c, v):
        sel_s[c] = v

    _topk_select(lambda c: keys_s[c], put_sel, nch, float(k_sel), 8, S_CH)
    o = _attn_online(_pad_heads(qa_ref[0]), lambda c: k_b[c].astype(BF16), lambda c: v_b[c].astype(BF16),
                     lambda c: sel_s[c][0:1, :], nch)
    o_ref[0] = o[0:8]


def _dsa_sample(page_table, iq, iw, qa, ikn, kan, van, c_ik, c_k, c_v, *, layer, past):
    n = page_table.shape[0]
    n_pages = page_table.shape[1]
    page = c_k.shape[-1]
    nch = (past + LANES) // S_CH
    per_seq = lambda width, lanes: pl.BlockSpec((1, width, lanes), lambda b, pt: (b, 0, 0))
    anyspec = pl.BlockSpec(memory_space=pl.ANY)
    grid_spec = pltpu.PrefetchScalarGridSpec(
        num_scalar_prefetch=1,
        grid=(n,),
        in_specs=[per_seq(8, IDX_DIM), per_seq(8, 1), per_seq(8, LANES),
                  per_seq(LANES, 1), per_seq(LANES, 1), per_seq(LANES, 1), anyspec, anyspec, anyspec],
        out_specs=per_seq(8, LANES),
        scratch_shapes=[pltpu.VMEM((nch, IDX_DIM, S_CH), F32), pltpu.VMEM((nch, LANES, S_CH), F32),
                        pltpu.VMEM((nch, LANES, S_CH), F32),
                        pltpu.VMEM((nch, 8, S_CH), jnp.int32), pltpu.VMEM((nch, 8, S_CH), F32),
                        pltpu.SemaphoreType.DMA((3,))],
    )
    return pl.pallas_call(
        functools.partial(_dsa_sample_kernel, layer=layer, past=past, n_pages=n_pages, page=page,
                          k_sel=min(DSA_TOPK, (past + 1) // 4)),
        grid_spec=grid_spec,
        out_shape=jax.ShapeDtypeStruct((n, 8, LANES), F32),
        compiler_params=_cparams(("arbitrary",)),
        name="dsa_sample",
    )(page_table, iq, iw, qa, ikn, kan, van, c_ik, c_k, c_v)


def _nsa_sample_kernel(pt_ref, qb_ref, qbr_ref, g_ref, skn_ref, svn_ref, wkn_ref, wvn_ref, wcmpk_ref, wcmpv_ref,
                       ov_ref, swk_ref, swv_ref, cck_hbm, ccv_hbm, csk_hbm, csv_hbm,
                       o_ref, nwk_ref, nwv_ref, ckT_b, cvT_b, ck_b, cv_b, sk_b, sv_b, sems,
                       *, layer, past, n_pages, page, n_cmp):
    b = pl.program_id(0)
    copies = _start_all(
        _page_copies_flat(pt_ref, b, layer, cck_hbm, ckT_b, sems.at[0], n_pages, page)
        + _page_copies_flat(pt_ref, b, layer, ccv_hbm, cvT_b, sems.at[1], n_pages, page)
        + _page_copies_chunked(pt_ref, b, layer, csk_hbm, sk_b, sems.at[2], n_pages, page, True)
        + _page_copies_chunked(pt_ref, b, layer, csv_hbm, sv_b, sems.at[3], n_pages, page, True))
    _set_new_column(sk_b, past, skn_ref[0])
    _set_new_column(sv_b, past, svn_ref[0])
    wb = swk_ref.shape[3]
    lane_w = lax.broadcasted_iota(jnp.int32, (LANES, LANES), 1)
    new_k = jnp.where(lane_w == 0, jnp.broadcast_to(wkn_ref[0], (LANES, LANES)), 0.0)
    new_v = jnp.where(lane_w == 0, jnp.broadcast_to(wvn_ref[0], (LANES, LANES)), 0.0)
    kw = jnp.concatenate([swk_ref[0, 0], new_k], axis=1).astype(BF16)
    vw = jnp.concatenate([swv_ref[0, 0], new_v], axis=1).astype(BF16)
    last = lax.broadcasted_iota(jnp.int32, (LANES, wb), 1) == wb - 1
    nwk_ref[0, 0] = jnp.where(last, jnp.broadcast_to(wkn_ref[0], (LANES, wb)), pltpu.roll(swk_ref[0, 0], wb - 1, 1))
    nwv_ref[0, 0] = jnp.where(last, jnp.broadcast_to(wvn_ref[0], (LANES, wb)), pltpu.roll(swv_ref[0, 0], wb - 1, 1))
    for cp in copies:
        cp.wait()

    for t in range(past // LANES):
        rows = slice(t * LANES, (t + 1) * LANES)
        ck_b[rows, :] = ckT_b[:, rows].T
        cv_b[rows, :] = cvT_b[:, rows].T
    n_cmp_rows = past // CMP_STRIDE
    n_slc = -(-(past + 1) // SLC_BLK)
    kc = _compress_rows(ck_b, wcmpk_ref, n_cmp_rows).astype(BF16)
    vc = _compress_rows(cv_b, wcmpv_ref, n_cmp_rows).astype(BF16)

    q16 = _pad_heads(qb_ref[0])
    qr16 = _pad_heads(qbr_ref[0])
    qpos = jnp.full((S_ROWS, 1), past, jnp.int32)
    cidx = lax.broadcasted_iota(jnp.int32, (S_ROWS, n_cmp_rows), 1)
    vis_c = (cidx * CMP_STRIDE + CMP_LEN - 1 <= qpos) & (cidx < n_cmp)
    p_c = _softmax_once(_dot_nt(q16, kc), vis_c)
    o_c = _dot(p_c.astype(BF16), vc)
    row = lax.broadcasted_iota(jnp.int32, p_c.shape, 0)
    g0 = jnp.sum(jnp.where(row < 4, p_c, 0.0), axis=0, keepdims=True)
    g1 = jnp.sum(jnp.where((row >= 4) & (row < 8), p_c, 0.0), axis=0, keepdims=True)
    p_sum = jnp.where(row < 4, jnp.broadcast_to(g0, p_c.shape), jnp.broadcast_to(g1, p_c.shape))
    keys = _block_keys(_dot_f32_exact_rhs(p_sum, ov_ref[...]), qpos, n_slc)
    sel_box = []
    _topk_select(lambda c: keys, lambda c, v: sel_box.append(v), 1, float(min(SLC_TOPN, n_slc)), S_ROWS,
                 keys.shape[1])
    sel_b = sel_box[0]

    def slc_mask(c):
        kpos = c * S_CH + lax.broadcasted_iota(jnp.int32, (S_ROWS, S_CH), 1)
        return jnp.where(kpos <= qpos, _expand_blocks(sel_b, c, S_CH), 0.0)

    o_s = _attn_online(qr16, lambda c: sk_b[c].astype(BF16), lambda c: sv_b[c].astype(BF16), slc_mask,
                       sk_b.shape[0])
    pos_w = past - wb + lax.broadcasted_iota(jnp.int32, (S_ROWS, wb + LANES), 1)
    dist = qpos - pos_w
    vis_w = (pos_w >= 0) & (dist >= 0) & (dist <= WINDOW)
    p_w = _softmax_once(_dot(qr16, kw), vis_w)
    o_w = _dot_nt(p_w.astype(BF16), vw)
    g = g_ref[0]
    o_ref[0] = g[:, 0:1] * o_c[0:8] + g[:, 1:2] * o_s[0:8] + g[:, 2:3] * o_w[0:8]


def _nsa_sample(page_table, qb, qbr, g, skn, svn, wkn, wvn, w_ck, w_cv, ov, s_wk, s_wv, c_ck, c_cv, c_sk, c_sv,
                *, layer, past):
    n = page_table.shape[0]
    n_pages = page_table.shape[1]
    page = c_ck.shape[-1]
    wb = s_wk.shape[3]
    nch = (past + LANES) // S_CH
    n_cmp = (past + 1 - CMP_LEN) // CMP_STRIDE + 1
    per_seq = lambda width, lanes: pl.BlockSpec((1, width, lanes), lambda b, pt: (b, 0, 0))
    fixed = lambda shape: pl.BlockSpec(shape, lambda b, pt: (0,) * len(shape))
    state = pl.BlockSpec((1, 1, LANES, wb), lambda b, pt: (layer, b, 0, 0))
    new_state = pl.BlockSpec((1, 1, LANES, wb), lambda b, pt: (0, b, 0, 0))
    anyspec = pl.BlockSpec(memory_space=pl.ANY)
    grid_spec = pltpu.PrefetchScalarGridSpec(
        num_scalar_prefetch=1,
        grid=(n,),
        in_specs=[per_seq(8, LANES), per_seq(8, LANES), per_seq(8, 3)] + [per_seq(LANES, 1)] * 4
        + [fixed((CMP_LEN, LANES)), fixed((CMP_LEN, LANES)), fixed(ov.shape), state, state]
        + [anyspec] * 4,
        out_specs=[per_seq(8, LANES), new_state, new_state],
        scratch_shapes=[pltpu.VMEM((LANES, past), F32)] * 2 + [pltpu.VMEM((past, LANES), F32)] * 2
        + [pltpu.VMEM((nch, LANES, S_CH), F32)] * 2 + [pltpu.SemaphoreType.DMA((4,))],
    )
    st_shape = jax.ShapeDtypeStruct((1, n, LANES, wb), F32)
    return pl.pallas_call(
        functools.partial(_nsa_sample_kernel, layer=layer, past=past, n_pages=n_pages, page=page, n_cmp=n_cmp),
        grid_spec=grid_spec,
        out_shape=[jax.ShapeDtypeStruct((n, 8, LANES), F32), st_shape, st_shape],
        compiler_params=_cparams(("arbitrary",)),
        name="nsa_sample",
    )(page_table, qb, qbr, g, skn, svn, wkn, wvn, w_ck, w_cv, ov, s_wk, s_wv, c_ck, c_cv, c_sk, c_sv)


def _merge_kernel(x_ref, oa_ref, ob_ref, gate_ref, wpa_ref, wpb_ref, wo_ref, o_ref):
    gate = gate_ref[...]
    m = (gate[:, :D_MODEL] * _dot(oa_ref[...], wpa_ref[...]) + gate[:, D_MODEL:] * _dot(ob_ref[...], wpb_ref[...]))
    o_ref[...] = x_ref[...] + _dot(m.astype(BF16), wo_ref[...])


def _merge(x, o_a, o_b, gate, w_pa, w_pb, w_o, *, tm):
    T = x.shape[0]
    row = lambda i: (i, 0)
    fixed = lambda i: (0, 0)
    return pl.pallas_call(
        _merge_kernel,
        grid=(T // tm,),
        in_specs=[pl.BlockSpec((tm, D_MODEL), row), pl.BlockSpec((tm, 512), row), pl.BlockSpec((tm, 512), row),
                  pl.BlockSpec((tm, 2 * D_MODEL), row), pl.BlockSpec((512, D_MODEL), fixed),
                  pl.BlockSpec((512, D_MODEL), fixed), pl.BlockSpec((D_MODEL, D_MODEL), fixed)],
        out_specs=pl.BlockSpec((tm, D_MODEL), row),
        out_shape=jax.ShapeDtypeStruct((T, D_MODEL), F32),
        compiler_params=_cparams(("arbitrary",)),
        name="merge",
    )(x, o_a, o_b, gate, w_pa, w_pb, w_o)


def _align_w_in(w):
    z = lambda n: jnp.zeros((w.shape[0], n), w.dtype)
    return jnp.concatenate([w[:, :1352], z(56), w[:, 1352:2632], w[:, 2632:2656], z(104), w[:, 2656:]], axis=1)


def _heads_from_rows(o):
    n = o.shape[0]
    return jnp.concatenate([o[:, :4, :HEAD_DIM].reshape(n, 256), o[:, 4:, HEAD_DIM:].reshape(n, 256)], axis=1)


def _positions_last(x):
    lead = x.shape[:-3]
    nd = len(lead)
    x = jnp.transpose(x, tuple(range(nd)) + (nd + 1, nd + 2, nd))
    return x.reshape(lead + (x.shape[-3] * x.shape[-2], x.shape[-1]))


def _positions_first(x, kv):
    lead = x.shape[:-2]
    nd = len(lead)
    x = x.reshape(lead + (kv, x.shape[-2] // kv, x.shape[-1]))
    return jnp.transpose(x, tuple(range(nd)) + (nd + 2, nd, nd + 1))


def kernel(x_prompt, x_sample, cache_dsa_k, cache_dsa_v, cache_dsa_idx_k, cache_nsa_cmp_k, cache_nsa_cmp_v, cache_nsa_slc_k, cache_nsa_slc_v, state_nsa_win_k, state_nsa_win_v, page_table, norm_ffn1, w_ffn1_up, w_ffn1_down, norm_mix, w_in, w_cmp_k, w_cmp_v, w_proj_a, w_proj_b, w_out, norm_ffn2, w_ffn2_up, w_ffn2_down, norm_final):
    n_p, s_len, _ = x_prompt.shape
    n_s, t_s, _ = x_sample.shape
    assert t_s == 1 and s_len % P_CH == 0
    depth, n_pool, page = cache_dsa_k.shape[:3]
    past = page_table.shape[1] * page
    wb = state_nsa_win_k.shape[2]
    assert (past + LANES) % S_CH == 0 and wb == WINDOW and page == LANES

    xp = x_prompt.reshape(n_p * s_len, D_MODEL)
    xs = x_sample.reshape(n_s, D_MODEL)
    tab_p = _rope_tables(jnp.arange(s_len))
    tab_s = _rope_tables(jnp.full((n_s,), past))
    grouped = lambda c: jnp.transpose(c, (0, 1, 3, 4, 2))
    c_k, c_v = grouped(cache_dsa_k), grouped(cache_dsa_v)
    c_ck, c_cv, c_sk, c_sv = (grouped(cache_nsa_cmp_k), grouped(cache_nsa_cmp_v), grouped(cache_nsa_slc_k),
                              grouped(cache_nsa_slc_v))
    c_ik = jnp.transpose(cache_dsa_idx_k, (0, 1, 3, 2))
    s_wk = _positions_last(state_nsa_win_k)
    s_wv = _positions_last(state_nsa_win_v)
    ov_p = _slc_overlap((s_len - CMP_LEN) // CMP_STRIDE + 1, s_len, s_len // CMP_STRIDE, LANES)
    ov_s = _slc_overlap((past + 1 - CMP_LEN) // CMP_STRIDE + 1, past + 1, past // CMP_STRIDE, 2 * LANES)
    row = lambda v: v.reshape(1, -1)
    bf = lambda w: w.astype(BF16)

    st_p, st_s = [], []
    for l in range(depth):
        w_ck = w_cmp_k[l].reshape(CMP_LEN, LANES)
        w_cv = w_cmp_v[l].reshape(CMP_LEN, LANES)
        w_al = bf(_align_w_in(w_in[l]))
        wu1, wd1, wu2, wd2 = bf(w_ffn1_up[l]), bf(w_ffn1_down[l]), bf(w_ffn2_up[l]), bf(w_ffn2_down[l])
        w_pa, w_pb, w_o = bf(w_proj_a[l]), bf(w_proj_b[l]), bf(w_out[l])
        g_fin = row(norm_final)
        last = l == depth - 1

        xp = _ffn(xp, row(norm_ffn1[l]), wu1, wd1, g_fin, tm=512, tf=D_FF // 2, final=False)
        pp = _proj(xp, row(norm_mix[l]), w_al, tab_p, _PROJ_PROMPT_OUTS, tm=256, seq=s_len)
        o_a = _dsa_prompt(pp["iqT"], pp["qaT"], pp["iwT"], pp["ikw"], pp["ka"], pp["vaT"], n=n_p, s=s_len)
        kc, vc = _compress_prompt(pp["ck"], pp["cv"], w_ck, w_cv, n=n_p, s=s_len)
        o_b = _nsa_prompt(pp["qbT"], pp["qbrT"], pp["gbT"], kc, vc, ov_p.T, pp["sk"], pp["svT"], pp["wk"],
                          pp["wvT"], n=n_p, s=s_len)
        xp = _merge(xp, o_a, o_b, pp["gate"], w_pa, w_pb, w_o, tm=512)
        xp = _ffn(xp, row(norm_ffn2[l]), wu2, wd2, g_fin, tm=512, tf=D_FF // 2, final=last)
        st_p.append((pp["kaT"], pp["vaT"], pp["ikT"], pp["ckT"], pp["cvT"], pp["skT"], pp["svT"],
                     pp["wkT"][:, :, s_len - wb:], pp["wvT"][:, :, s_len - wb:]))

        xs = _ffn(xs, row(norm_ffn1[l]), wu1, wd1, g_fin, tm=n_s, tf=D_FF // 2, final=False)
        ps = _proj(xs, row(norm_mix[l]), w_al, tab_s, _PROJ_SAMPLE_OUTS, tm=n_s, seq=n_s)
        heads = lambda q: q.astype(F32).reshape(n_s, 8, HEAD_DIM)
        dup = lambda q: jnp.tile(heads(q), (1, 1, 2))
        col = lambda t: t.reshape(n_s, LANES, 1)
        o_a = _dsa_sample(page_table, heads(ps["iq"]), ps["ikw"][:, IDX_DIM:IDX_DIM + IDX_HEADS].reshape(n_s, 8, 1),
                          dup(ps["qa"]), col(ps["ikw"]), col(ps["ka"]), col(ps["va"]), c_ik, c_k, c_v,
                          layer=l, past=past)
        o_b, nwk, nwv = _nsa_sample(page_table, dup(ps["qb"]), dup(ps["qbr"]),
                                    ps["gb"][:, :3 * B_HEADS].reshape(n_s, 8, 3),
                                    col(ps["sk"]), col(ps["sv"]), col(ps["wk"]), col(ps["wv"]), w_ck, w_cv, ov_s,
                                    s_wk, s_wv, c_ck, c_cv, c_sk, c_sv, layer=l, past=past)
        xs = _merge(xs, bf(_heads_from_rows(o_a)), bf(_heads_from_rows(o_b)), ps["gate"], w_pa, w_pb, w_o, tm=n_s)
        xs = _ffn(xs, row(norm_ffn2[l]), wu2, wd2, g_fin, tm=n_s, tf=D_FF // 2, final=last)
        kv1 = lambda t: t.reshape(n_s, 1, 2, HEAD_DIM)
        st_s.append((kv1(ps["ka"]), kv1(ps["va"]), ps["ikw"][:, :IDX_DIM].reshape(n_s, 1, IDX_DIM), kv1(ps["ck"]),
                     kv1(ps["cv"]), kv1(ps["sk"]), kv1(ps["sv"]), nwk[0], nwv[0]))

    outs = [xp.reshape(n_p, s_len, D_MODEL), xs.reshape(n_s, 1, D_MODEL)]
    for i in range(9):
        sp = jnp.stack([s[i] for s in st_p])
        ss = jnp.stack([s[i] for s in st_s])
        if i == 2:
            sp = jnp.transpose(sp, (0, 1, 3, 2))
        else:
            sp = _positions_first(sp, 2)
        if i >= 7:
            ss = _positions_first(ss, 2)
        outs += [sp, ss]
    return tuple(outs)
```

```python
import functools

import jax
import jax.numpy as jnp
import numpy as np
from jax import lax
from jax.experimental import pallas as pl
from jax.experimental.pallas import tpu as pltpu

D_MODEL = 1024
HEAD_DIM = 64
ROT_DIM = HEAD_DIM // 4
ROPE_THETA = 500000.0
A_KV = 2
IDX_HEADS = 8
IDX_DIM = 64
DSA_TOPK = 256
B_HEADS = 8
B_KV = 2
CMP_STRIDE = 16
CMP_LEN = 2 * CMP_STRIDE
SLC_BLK = 64
SLC_TOPN = 16
WINDOW = 512
D_FF = 2816
Q_BLOCK = 128
EPS = 1e-6
NEG = -1e30
INT_MIN = -(2 ** 31)

LANES = 128
VMEM_LIMIT = 56 * 1024 * 1024

F32 = jnp.float32
BF16 = jnp.bfloat16

C_QA, C_KA, C_VA, C_IQ, C_IKW, C_QB = 0, 512, 640, 768, 1280, 1408
C_CK, C_CV, C_SK, C_SV, C_WK, C_WV, C_GB, C_GATE = 1920, 2048, 2176, 2304, 2432, 2560, 2688, 2816
D_IN_ALIGNED = C_GATE + 2 * D_MODEL


def _cparams(sem):
    return pltpu.CompilerParams(dimension_semantics=sem, vmem_limit_bytes=VMEM_LIMIT)


def _rms(x, g):
    return x * lax.rsqrt(jnp.mean(x * x, axis=-1, keepdims=True) + EPS) * g


def _dot(a, b):
    return jnp.dot(a, b, preferred_element_type=F32)


def _dot_nt(a, b):
    return lax.dot_general(a, b, (((1,), (1,)), ((), ())), preferred_element_type=F32)


def _loop(n, body, init):
    if isinstance(n, int):
        carry = init
        for j in range(n):
            carry = body(j, carry)
        return carry
    return lax.fori_loop(0, n, body, init)


def _ffn_kernel(x_ref, g_ref, wa_ref, wb_ref, wd_ref, gf_ref, o_ref, h_s, acc_s, *, n_ff, final, row_split):
    j = pl.program_id(1)

    @pl.when(j == 0)
    def _():
        h_s[...] = _rms(x_ref[...], g_ref[...]).astype(BF16)
        acc_s[...] = jnp.zeros_like(acc_s)

    rows = h_s.shape[0] // row_split
    for r in range(row_split):
        rs = slice(r * rows, (r + 1) * rows)
        h = h_s[rs, :]
        a = _dot(h, wa_ref[...])
        b = _dot(h, wb_ref[...])
        act = (a * jax.nn.sigmoid(a)) * b
        acc_s[rs, :] += _dot(act.astype(BF16), wd_ref[...])

    @pl.when(j == n_ff - 1)
    def _():
        y = x_ref[...] + 0.5 * acc_s[...]
        if final:
            y = _rms(y, gf_ref[...])
        o_ref[...] = y


def _ffn(x, g, w_up, w_down, g_final, *, tm, tf, final):
    T = x.shape[0]
    n_ff = D_FF // tf
    row_split = max(1, tm // 512)
    return pl.pallas_call(
        functools.partial(_ffn_kernel, n_ff=n_ff, final=final, row_split=row_split),
        grid=(T // tm, n_ff),
        in_specs=[
            pl.BlockSpec((tm, D_MODEL), lambda i, j: (i, 0)),
            pl.BlockSpec((1, D_MODEL), lambda i, j: (0, 0)),
            pl.BlockSpec((D_MODEL, tf), lambda i, j: (0, j)),
            pl.BlockSpec((D_MODEL, tf), lambda i, j: (0, j + n_ff)),
            pl.BlockSpec((tf, D_MODEL), lambda i, j: (j, 0)),
            pl.BlockSpec((1, D_MODEL), lambda i, j: (0, 0)),
        ],
        out_specs=pl.BlockSpec((tm, D_MODEL), lambda i, j: (i, 0)),
        out_shape=jax.ShapeDtypeStruct((T, D_MODEL), F32),
        scratch_shapes=[pltpu.VMEM((tm, D_MODEL), BF16), pltpu.VMEM((tm, D_MODEL), F32)],
        compiler_params=_cparams(("arbitrary", "arbitrary")),
        name="ffn",
    )(x, g, w_up, w_up, w_down, g_final)


def _rope_tables(pos):
    half = ROT_DIM // 2
    inv = ROPE_THETA ** (-jnp.arange(half, dtype=F32) / half)
    ang = pos.astype(F32)[:, None] * inv
    cos, sin = jnp.cos(ang), jnp.sin(ang)
    n = pos.shape[0]
    one, zero = jnp.ones((n, HEAD_DIM - ROT_DIM), F32), jnp.zeros((n, HEAD_DIM - ROT_DIM), F32)
    z8 = jnp.zeros((n, half), F32)
    c = jnp.concatenate([cos, cos, one], axis=1)
    slo = jnp.concatenate([-sin, z8, zero], axis=1)
    shi = jnp.concatenate([z8, sin, zero], axis=1)
    return tuple(jnp.concatenate([t, t], axis=1) for t in (c, slo, shi))


_PROJ_SAMPLE_OUTS = (
    [("tok", n, 512, BF16) for n in ("qa", "iq", "qb", "qbr")]
    + [("tok", n, LANES, F32) for n in ("ikw", "ka", "va", "ck", "cv", "sk", "sv", "wk", "wv", "gb")]
    + [("tok", "gate", 2 * D_MODEL, F32)])
_PROJ_PROMPT_OUTS = (
    [("tok", n, LANES, F32) for n in ("ikw", "ka", "sk", "wk", "ck", "cv")]
    + [("tok", "gate", 2 * D_MODEL, F32)]
    + [("chan", n, 512, BF16) for n in ("qaT", "iqT", "qbT", "qbrT")]
    + [("chan", "iwT", IDX_HEADS, F32), ("chan", "gbT", 3 * B_HEADS, F32), ("chan", "ikT", IDX_DIM, F32)]
    + [("chan", n, LANES, F32) for n in ("kaT", "vaT", "ckT", "cvT", "skT", "svT", "wkT", "wvT")])


def _proj_kernel(x_ref, g_ref, w_ref, c_ref, slo_ref, shi_ref, *out_refs, outs):
    o = {name: ref for (_, name, _, _), ref in zip(outs, out_refs)}
    h = _rms(x_ref[...], g_ref[...]).astype(BF16)
    C, SLO, SHI = c_ref[...], slo_ref[...], shi_ref[...]
    scale = HEAD_DIM ** -0.5

    def mm(c0, width):
        return _dot(h, w_ref[:, c0:c0 + width])

    def rope(p):
        outs = []
        for j in range(p.shape[1] // LANES):
            xj = p[:, j * LANES:(j + 1) * LANES]
            outs.append(xj * C + pltpu.roll(xj, LANES - ROT_DIM // 2, 1) * SLO
                        + pltpu.roll(xj, ROT_DIM // 2, 1) * SHI)
        return outs[0] if len(outs) == 1 else jnp.concatenate(outs, axis=1)

    def put(name, v):
        if name in o:
            o[name][...] = v.astype(o[name].dtype)
        if name + "T" in o:
            ref = o[name + "T"]
            ref[0] = v.T[0:ref.shape[1], :].astype(ref.dtype)

    put("qa", rope(mm(C_QA, 512)) * scale)
    put("iq", rope(mm(C_IQ, 512)) * (IDX_DIM ** -0.5))
    qb = mm(C_QB, 512)
    put("qb", qb * scale)
    put("qbr", rope(qb) * scale)
    ikw = mm(C_IKW, 128)
    lane = lax.broadcasted_iota(jnp.int32, ikw.shape, 1)
    ikw = jnp.where(lane < IDX_DIM, rope(ikw), ikw)
    put("ikw", ikw)
    if "ikT" in o:
        ikw_t = ikw.T
        o["ikT"][0] = ikw_t[0:IDX_DIM, :]
        o["iwT"][0] = ikw_t[IDX_DIM:IDX_DIM + IDX_HEADS, :]
    put("gb", jax.nn.sigmoid(mm(C_GB, 128)))
    put("gate", jax.nn.sigmoid(mm(C_GATE, 2 * D_MODEL)))
    put("ck", mm(C_CK, 128))
    put("cv", mm(C_CV, 128))
    put("ka", rope(mm(C_KA, 128)))
    put("va", mm(C_VA, 128))
    put("sk", rope(mm(C_SK, 128)))
    put("sv", mm(C_SV, 128))
    put("wk", rope(mm(C_WK, 128)))
    put("wv", mm(C_WV, 128))


def _proj(x, g, w_al, tables, outs, *, tm, seq):
    T = x.shape[0]
    per_seq = seq // tm
    row = lambda i: (i, 0)
    fixed = lambda i: (0, 0)
    tab = pl.BlockSpec((tm, LANES), lambda i: (i % per_seq, 0))
    shapes, specs = [], []
    for kind, _, ch, dt in outs:
        if kind == "tok":
            shapes.append(jax.ShapeDtypeStruct((T, ch), dt))
            specs.append(pl.BlockSpec((tm, ch), row))
        else:
            shapes.append(jax.ShapeDtypeStruct((T // seq, ch, seq), dt))
            specs.append(pl.BlockSpec((1, ch, tm), lambda i: (i // per_seq, 0, i % per_seq)))
    res = pl.pallas_call(
        functools.partial(_proj_kernel, outs=tuple(outs)),
        grid=(T // tm,),
        in_specs=[pl.BlockSpec((tm, D_MODEL), row), pl.BlockSpec((1, D_MODEL), fixed),
                  pl.BlockSpec((D_MODEL, D_IN_ALIGNED), fixed), tab, tab, tab],
        out_specs=specs,
        out_shape=shapes,
        compiler_params=_cparams(("arbitrary",)),
        name="proj",
    )(x, g, w_al, *tables)
    return {name: r for (_, name, _, _), r in zip(outs, res)}


def _sort_key(score):
    bits = lax.bitcast_convert_type(score, jnp.int32)
    key = jnp.where(bits < 0, bits ^ jnp.int32(0x7FFFFFFF), bits)
    return jnp.where(score == 0.0, jnp.int32(0), key)


def _fold_lanes(x):
    acc = x[:, :LANES]
    for j in range(1, x.shape[1] // LANES):
        acc = acc + x[:, j * LANES:(j + 1) * LANES]
    return acc


def _count(get_keys, nch, rows, pred):
    def body(c, acc):
        return acc + _fold_lanes(jnp.where(pred(get_keys(c)), 1.0, 0.0))
    acc = _loop(nch, body, jnp.zeros((rows, LANES), F32))
    return jnp.sum(acc, axis=-1, keepdims=True)


def _kth_largest(get_keys, nch, k, rows):
    def bit_body(b, t):
        trial = t + lax.shift_left(jnp.int32(1), jnp.int32(31) - b)
        cnt = _count(get_keys, nch, rows, lambda kc: kc >= trial)
        return jnp.where(cnt >= k, trial, t)
    return lax.fori_loop(0, 32, bit_body, jnp.full((rows, 1), INT_MIN, jnp.int32))


def _prefix_matrix():
    r = lax.broadcasted_iota(jnp.int32, (LANES, LANES), 0)
    c = lax.broadcasted_iota(jnp.int32, (LANES, LANES), 1)
    return jnp.where(r <= c, 1.0, 0.0).astype(BF16)


def _topk_select(get_keys, put_sel, nch, k, rows, ch):
    t = _kth_largest(get_keys, nch, k, rows)
    need = k - _count(get_keys, nch, rows, lambda kc: kc > t)
    tri = _prefix_matrix()

    def body(c, run):
        kc = get_keys(c)
        sels = []
        for j in range(ch // LANES):
            kj = kc[:, j * LANES:(j + 1) * LANES]
            eq = kj == t
            pj = _dot(jnp.where(eq, 1.0, 0.0).astype(BF16), tri)
            take = (kj > t) | (eq & ((pj + run) <= need))
            sels.append(jnp.where(take & (kj != INT_MIN), 1.0, 0.0))
            run = run + pj[:, LANES - 1:LANES]
        put_sel(c, sels[0] if len(sels) == 1 else jnp.concatenate(sels, axis=1))
        return run

    _loop(nch, body, jnp.zeros((rows, 1), F32))


def _attn_all_chunks(q, get_k, get_v, get_mask, nch):
    s = jnp.concatenate([_dot(q, get_k(c)) + jnp.where(get_mask(c) > 0.5, 0.0, NEG) for c in range(nch)], axis=1)
    p = jnp.exp(s - jnp.max(s, axis=-1, keepdims=True))
    l = jnp.sum(p, axis=-1, keepdims=True)
    ch = s.shape[1] // nch
    acc = _dot_nt(p[:, 0:ch].astype(BF16), get_v(0))
    for c in range(1, nch):
        acc = acc + _dot_nt(p[:, c * ch:(c + 1) * ch].astype(BF16), get_v(c))
    return acc / l


def _softmax_once(s, keep):
    s = jnp.where(keep, s, NEG)
    e = jnp.where(keep, jnp.exp(s - jnp.max(s, axis=-1, keepdims=True)), 0.0)
    l = jnp.sum(e, axis=-1, keepdims=True)
    return e / jnp.where(l > 0.0, l, 1.0)


def _dot_f32_exact_rhs(p, w_bf16):
    p1 = p.astype(BF16)
    r1 = p - p1.astype(F32)
    p2 = r1.astype(BF16)
    p3 = (r1 - p2.astype(F32)).astype(BF16)
    return _dot(p1, w_bf16) + _dot(p2, w_bf16) + _dot(p3, w_bf16)


def _head_queries(qT):
    zeros = jnp.zeros((HEAD_DIM, qT.shape[1]), qT.dtype)
    tiles = []
    for h in range(8):
        x = qT[h * HEAD_DIM:(h + 1) * HEAD_DIM, :]
        tiles.append(jnp.concatenate([x, zeros] if h < 4 else [zeros, x], axis=0))
    return jnp.concatenate(tiles, axis=1).astype(BF16)


def _per_head(x, heads, fn):
    r_w = x.shape[1] // heads
    return jnp.concatenate([fn(r, x[:, r * r_w:(r + 1) * r_w]) for r in range(heads)], axis=1)


SHIFT_RANGE_FLOOR = 1e-30


def _group_norm2_max(k_bf16):
    kf = k_bf16.astype(F32)
    sq = kf * kf
    lane = lax.broadcasted_iota(jnp.int32, sq.shape, 1)
    n0 = jnp.max(jnp.sum(jnp.where(lane < HEAD_DIM, sq, 0.0), axis=1, keepdims=True), axis=0, keepdims=True)
    n1 = jnp.max(jnp.sum(jnp.where(lane >= HEAD_DIM, sq, 0.0), axis=1, keepdims=True), axis=0, keepdims=True)
    out_lane = lax.broadcasted_iota(jnp.int32, (1, LANES), 1)
    return jnp.where(out_lane < HEAD_DIM, n0, n1)


def _per_head_row(n2, r_w):
    return jnp.concatenate([jnp.broadcast_to(n2[:, 0:1], (1, 4 * r_w)),
                            jnp.broadcast_to(n2[:, HEAD_DIM:HEAD_DIM + 1], (1, 4 * r_w))], axis=1)


def _attn_keys_on_rows(qT, get_k, get_vT, get_bias, nch, heads, k_norm2_max):
    width = qT.shape[1]
    r_w = width // heads

    def biased(c):
        biases = get_bias(c)
        return lambda r, t: t + biases[r * len(biases) // heads]

    qf = qT.astype(F32)
    shift = jnp.sqrt(jnp.sum(qf * qf, axis=0, keepdims=True) * k_norm2_max)

    def fixed_body(c, carry):
        l, acc = carry
        add_bias = biased(c)
        p = _per_head(_dot(get_k(c), qT), heads,
                      lambda r, t: jnp.exp(add_bias(r, t) - shift[:, r * r_w:(r + 1) * r_w]))
        return l + jnp.sum(p, axis=0, keepdims=True), acc + _dot(get_vT(c), p.astype(BF16))

    l, acc = _loop(nch, fixed_body, (jnp.zeros((1, width), F32), jnp.zeros((LANES, width), F32)))
    in_range = jnp.min(l) > SHIFT_RANGE_FLOOR

    def running_max():
        def body(c, carry):
            m, l, acc = carry
            s = _per_head(_dot(get_k(c), qT), heads, biased(c))
            m_new = jnp.maximum(m, jnp.max(s, axis=0, keepdims=True))
            alpha = jnp.exp(m - m_new)
            p = jnp.exp(s - m_new)
            l = alpha * l + jnp.sum(p, axis=0, keepdims=True)
            acc = alpha * acc + _dot(get_vT(c), p.astype(BF16))
            return m_new, l, acc

        init = (jnp.full((1, width), NEG, F32), jnp.zeros((1, width), F32), jnp.zeros((LANES, width), F32))
        _, l2, acc2 = _loop(nch, body, init)
        return acc2 / l2

    return lax.cond(in_range, lambda: acc / l, running_max)


def _softmax_keys_on_rows(s, keep, heads):
    def one(r, t):
        t = jnp.where(keep, t, NEG)
        e = jnp.where(keep, jnp.exp(t - jnp.max(t, axis=0, keepdims=True)), 0.0)
        l = jnp.sum(e, axis=0, keepdims=True)
        return e / jnp.where(l > 0.0, l, 1.0)
    return _per_head(s, heads, one)


def _heads_to_tokens(o):
    r_w = o.shape[1] // 8
    tiles = []
    for h in range(8):
        g = h // 4
        tiles.append(o[g * HEAD_DIM:(g + 1) * HEAD_DIM, h * r_w:(h + 1) * r_w])
    return jnp.concatenate(tiles, axis=0).T


def _lower_triangle(n):
    r = lax.broadcasted_iota(jnp.int32, (n, n), 0)
    c = lax.broadcasted_iota(jnp.int32, (n, n), 1)
    return jnp.where(c <= r, 1.0, 0.0).astype(BF16)


def _dot_f32_exact_lhs(w_bf16, p):
    p1 = p.astype(BF16)
    r1 = p - p1.astype(F32)
    p2 = r1.astype(BF16)
    p3 = (r1 - p2.astype(F32)).astype(BF16)
    return _dot(w_bf16, p1) + _dot(w_bf16, p2) + _dot(w_bf16, p3)


P_CH = 512
I16_MIN = -(2 ** 15)


def _count16(ref, nch, pred):
    rows = 16

    def body(c, acc):
        ind = jnp.where(pred(ref[c]), jnp.int16(1), jnp.int16(0))
        parts = [ind[j * rows:(j + 1) * rows] for j in range(ind.shape[0] // rows)]
        while len(parts) > 1:
            parts = [a + b for a, b in zip(parts[0::2], parts[1::2])]
        return acc + parts[0]

    acc = lax.fori_loop(0, nch, body, jnp.zeros((rows, ref.shape[2]), jnp.int16))
    return jnp.sum(acc.astype(F32), axis=0, keepdims=True)


def _kth_largest16(ref, nch, k):
    def bit_body(b, t):
        trial = t + lax.shift_left(jnp.int32(1), jnp.int32(15) - b)
        trial16 = trial.astype(jnp.int16)
        cnt = _count16(ref, nch, lambda x: x >= trial16)
        return jnp.where(cnt >= k, trial, t)
    return lax.fori_loop(0, 16, bit_body, jnp.full((1, ref.shape[2]), I16_MIN, jnp.int32))


def _topk_bias_keys_on_rows(keys_ref, hi_ref, lo_ref, bias_ref, nch, k, ch):
    h = _kth_largest16(hi_ref, nch, k)
    h16 = h.astype(jnp.int16)
    k_lo = k - _count16(hi_ref, nch, lambda x: x > h16)

    def narrow(c, carry):
        lo_ref[c] = jnp.where(hi_ref[c] == h16, lo_ref[c], jnp.int16(I16_MIN))
        return carry

    lax.fori_loop(0, nch, narrow, 0)
    t_lo = _kth_largest16(lo_ref, nch, k_lo)
    t_lo16 = t_lo.astype(jnp.int16)
    t = lax.shift_left(h, 16) + (t_lo + 32768)
    n_above = k - k_lo + _count16(lo_ref, nch, lambda x: x > t_lo16)
    n_tied = _count16(lo_ref, nch, lambda x: x == t_lo16)
    need = k - n_above
    has_split_tie = jnp.max(n_tied - need) > 0.0

    @pl.when(jnp.logical_not(has_split_tie))
    def _():
        def body(c, carry):
            kc = keys_ref[c]
            bias_ref[c] = jnp.where((kc >= t) & (kc != INT_MIN), 0.0, NEG)
            return carry
        lax.fori_loop(0, nch, body, 0)

    @pl.when(has_split_tie)
    def _():
        tri = _lower_triangle(LANES)

        def body(c, run):
            kc = keys_ref[c]
            tiles = []
            for j in range(ch // LANES):
                kj = kc[j * LANES:(j + 1) * LANES]
                eq = kj == t
                pref = _dot(tri, jnp.where(eq, 1.0, 0.0).astype(BF16)) + run
                take = (kj > t) | (eq & (pref <= need))
                tiles.append(jnp.where(take & (kj != INT_MIN), 0.0, NEG))
                run = pref[LANES - 1:LANES, :]
            bias_ref[c] = jnp.concatenate(tiles, axis=0)
            return run
        lax.fori_loop(0, nch, body, jnp.zeros((1, keys_ref.shape[2]), F32))


def _dsa_prompt_kernel(iqT_ref, qaT_ref, iwT_ref, ikw_ref, ka_ref, vT_ref, o_ref,
                       ik_s, k_s, v_s, kn_s, keys_s, hi_s, lo_s, bias_s, *, k_sel, n_chunks):
    i = pl.program_id(1)
    R = Q_BLOCK

    @pl.when(i == 0)
    def _():
        ik_s[...] = ikw_ref[:, 0:IDX_DIM].astype(BF16)
        k_s[...] = ka_ref[...].astype(BF16)
        kn_s[...] = jnp.broadcast_to(_group_norm2_max(k_s[...]), kn_s.shape)
        for c in range(n_chunks):
            v_s[c] = vT_ref[0, :, c * P_CH:(c + 1) * P_CH].astype(BF16)

    nch = lax.shift_right_logical(i, 2) + 1
    qpos = i * R + lax.broadcasted_iota(jnp.int32, (1, R), 1)
    key_rows = lambda ref, c: ref[pl.ds(pl.multiple_of(c * P_CH, P_CH), P_CH), :]

    iq_t = iqT_ref[0]
    iq_heads = jnp.concatenate([iq_t[h * IDX_DIM:(h + 1) * IDX_DIM, :] for h in range(IDX_HEADS)], axis=1)
    w_t = iwT_ref[0] * (IDX_HEADS ** -0.5)

    def score_body(c, carry):
        d = _dot(key_rows(ik_s, c), iq_heads)
        tot = jnp.zeros((P_CH, R), F32)
        for h in range(IDX_HEADS):
            tot = tot + w_t[h:h + 1, :] * jnp.maximum(d[:, h * R:(h + 1) * R], 0.0)
        kpos = c * P_CH + lax.broadcasted_iota(jnp.int32, (P_CH, R), 0)
        key = jnp.where(kpos <= qpos, _sort_key(tot), INT_MIN)
        keys_s[c] = key
        hi_s[c] = lax.shift_right_arithmetic(key, 16).astype(jnp.int16)
        lo_s[c] = ((key & 0xFFFF) - 32768).astype(jnp.int16)
        return carry

    lax.fori_loop(0, nch, score_body, 0)
    _topk_bias_keys_on_rows(keys_s, hi_s, lo_s, bias_s, nch, float(k_sel), P_CH)

    o = _attn_keys_on_rows(_head_queries(qaT_ref[0]), lambda c: key_rows(k_s, c), lambda c: v_s[c],
                           lambda c: [bias_s[c]], nch, 8, _per_head_row(kn_s[0:1, :], R))
    o_ref[...] = _heads_to_tokens(o).astype(o_ref.dtype)


def _dsa_prompt(iqT, qaT, iwT, ikw, ka, vaT, *, n, s):
    nq = s // Q_BLOCK
    n_chunks = s // P_CH
    k_sel = min(DSA_TOPK, s // 4)
    qrow = lambda b, i: (b * nq + i, 0)
    qcol = lambda b, i: (b, 0, i)
    tokens = lambda b, i: (b, 0)
    full = lambda b, i: (b, 0, 0)
    return pl.pallas_call(
        functools.partial(_dsa_prompt_kernel, k_sel=k_sel, n_chunks=n_chunks),
        grid=(n, nq),
        in_specs=[pl.BlockSpec((1, 512, Q_BLOCK), qcol), pl.BlockSpec((1, 512, Q_BLOCK), qcol),
                  pl.BlockSpec((1, IDX_HEADS, Q_BLOCK), qcol),
                  pl.BlockSpec((s, LANES), tokens), pl.BlockSpec((s, LANES), tokens),
                  pl.BlockSpec((1, LANES, s), full)],
        out_specs=pl.BlockSpec((Q_BLOCK, 512), qrow),
        out_shape=jax.ShapeDtypeStruct((n * s, 512), BF16),
        scratch_shapes=[pltpu.VMEM((s, IDX_DIM), BF16), pltpu.VMEM((s, LANES), BF16),
                        pltpu.VMEM((n_chunks, LANES, P_CH), BF16), pltpu.VMEM((8, LANES), F32),
                        pltpu.VMEM((n_chunks, P_CH, Q_BLOCK), jnp.int32),
                        pltpu.VMEM((n_chunks, P_CH, Q_BLOCK), jnp.int16),
                        pltpu.VMEM((n_chunks, P_CH, Q_BLOCK), jnp.int16),
                        pltpu.VMEM((n_chunks, P_CH, Q_BLOCK), F32)],
        compiler_params=_cparams(("arbitrary", "arbitrary")),
        name="dsa_prompt",
    )(iqT, qaT, iwT, ikw, ka, vaT)


def _compress_rows(src_ref, w_ref, n_chunks):
    lo = jnp.zeros((n_chunks, LANES), F32)
    hi = jnp.zeros((n_chunks, LANES), F32)
    for i in range(CMP_STRIDE):
        xi = src_ref[pl.ds(i, n_chunks, stride=CMP_STRIDE), :]
        lo = lo + xi * w_ref[i:i + 1, :]
        hi = hi + xi * w_ref[CMP_STRIDE + i:CMP_STRIDE + i + 1, :]
    return lo + pltpu.roll(hi, n_chunks - 1, 0)


def _compress_kernel(ck_ref, cv_ref, wk_ref, wv_ref, kc_o, vc_o, *, n_chunks):
    kc_o[...] = _compress_rows(ck_ref, wk_ref, n_chunks)
    vc_o[...] = _compress_rows(cv_ref, wv_ref, n_chunks)


def _compress_prompt(ck, cv, w_ck, w_cv, *, n, s):
    n_chunks = s // CMP_STRIDE
    full = lambda b: (b, 0)
    fixed = lambda b: (0, 0)
    shape = jax.ShapeDtypeStruct((n * n_chunks, LANES), F32)
    return pl.pallas_call(
        functools.partial(_compress_kernel, n_chunks=n_chunks),
        grid=(n,),
        in_specs=[pl.BlockSpec((s, LANES), full), pl.BlockSpec((s, LANES), full),
                  pl.BlockSpec((CMP_LEN, LANES), fixed), pl.BlockSpec((CMP_LEN, LANES), fixed)],
        out_specs=[pl.BlockSpec((n_chunks, LANES), full)] * 2,
        out_shape=[shape, shape],
        compiler_params=_cparams(("arbitrary",)),
        name="compress_prompt",
    )(ck, cv, w_ck, w_cv)


def _slc_overlap(n_cmp, n_keys, rows, cols):
    n_slc = -(-n_keys // SLC_BLK)
    j = np.arange(n_cmp)[:, None]
    sb = np.arange(n_slc)[None, :]
    lo = np.maximum(j * CMP_STRIDE, sb * SLC_BLK)
    hi = np.minimum(j * CMP_STRIDE + CMP_LEN, (sb + 1) * SLC_BLK)
    ov = np.zeros((rows, cols), np.float32)
    ov[:n_cmp, :n_slc] = np.maximum(hi - lo, 0) / CMP_STRIDE
    return jnp.asarray(ov, dtype=BF16)


def _block_keys(p_s, qpos, n_slc):
    blk = lax.broadcasted_iota(jnp.int32, p_s.shape, 1)
    cur = lax.shift_right_logical(qpos, 6)
    forced = (blk == 0) | (blk == cur) | (blk == cur - 1)
    p_s = jnp.where(forced, 1e6, p_s)
    p_s = jnp.where(blk * SLC_BLK <= qpos, p_s, NEG)
    return jnp.where(blk < n_slc, _sort_key(p_s), INT_MIN)


def _expand_blocks(sel_b, c, ch):
    nb = sel_b.shape[1]
    blk = lax.broadcasted_iota(jnp.int32, (nb, ch), 0)
    kblk = lax.shift_right_logical(c * ch + lax.broadcasted_iota(jnp.int32, (nb, ch), 1), 6)
    e = jnp.where(blk == kblk, 1.0, 0.0).astype(BF16)
    return _dot(sel_b.astype(BF16), e)


def _block_keys_on_rows(p_s, qpos, n_slc):
    blk = lax.broadcasted_iota(jnp.int32, p_s.shape, 0)
    cur = lax.shift_right_logical(qpos, 6)
    forced = (blk == 0) | (blk == cur) | (blk == cur - 1)
    p_s = jnp.where(forced, 1e6, p_s)
    p_s = jnp.where(blk * SLC_BLK <= qpos, p_s, NEG)
    return jnp.where(blk < n_slc, _sort_key(p_s), INT_MIN)


def _topk_mask_keys_on_rows(keys, k):
    count = lambda m: jnp.sum(jnp.where(m, 1.0, 0.0), axis=0, keepdims=True)

    def bit_body(b, t):
        trial = t + lax.shift_left(jnp.int32(1), jnp.int32(31) - b)
        return jnp.where(count(keys >= trial) >= k, trial, t)

    t = lax.fori_loop(0, 32, bit_body, jnp.full((1, keys.shape[1]), INT_MIN, jnp.int32))
    need = k - count(keys > t)
    eq = keys == t
    pref = _dot(_lower_triangle(keys.shape[0]), jnp.where(eq, 1.0, 0.0).astype(BF16))
    take = (keys > t) | (eq & (pref <= need))
    return jnp.where(take & (keys != INT_MIN), 1.0, 0.0)


def _nsa_prompt_kernel(qbT_ref, qbrT_ref, gbT_ref, kc_ref, vc_ref, ovT_ref, sk_ref, svT_ref, wk_ref, wvT_ref, o_ref,
                       kc_s, vc_s, sk_s, sv_s, wk_s, wv_s, kn_s, *, s_len, n_cmp):
    i = pl.program_id(1)
    R = Q_BLOCK
    n_cmp_rows = s_len // CMP_STRIDE
    n_slc = s_len // SLC_BLK
    w_pad = WINDOW // R
    w_tiles = WINDOW // R + 1

    @pl.when(i == 0)
    def _():
        kc_s[...] = kc_ref[...].astype(BF16)
        for t in range(n_cmp_rows // LANES):
            vc_s[:, t * LANES:(t + 1) * LANES] = vc_ref[t * LANES:(t + 1) * LANES, :].T.astype(BF16)
        sk_s[...] = sk_ref[...].astype(BF16)
        kn_s[...] = jnp.broadcast_to(_group_norm2_max(sk_s[...]), kn_s.shape)
        for c in range(s_len // P_CH):
            sv_s[c] = svT_ref[0, :, c * P_CH:(c + 1) * P_CH].astype(BF16)
        wk_s[0:WINDOW, :] = jnp.zeros((WINDOW, LANES), BF16)
        wk_s[WINDOW:WINDOW + s_len, :] = wk_ref[...].astype(BF16)
        for c in range(w_pad):
            wv_s[c] = jnp.zeros((LANES, R), BF16)
        for c in range(s_len // R):
            wv_s[w_pad + c] = wvT_ref[0, :, c * R:(c + 1) * R].astype(BF16)

    nch = lax.shift_right_logical(i, 2) + 1
    qpos = i * R + lax.broadcasted_iota(jnp.int32, (1, R), 1)
    key_rows = lambda ref, c: ref[pl.ds(pl.multiple_of(c * P_CH, P_CH), P_CH), :]
    q, qr, gates = _head_queries(qbT_ref[0]), _head_queries(qbrT_ref[0]), gbT_ref[0]

    cidx = lax.broadcasted_iota(jnp.int32, (n_cmp_rows, R), 0)
    vis_c = (cidx * CMP_STRIDE + CMP_LEN - 1 <= qpos) & (cidx < n_cmp)
    pos_w = i * R - WINDOW + lax.broadcasted_iota(jnp.int32, (WINDOW + R, R), 0)
    dist = qpos - pos_w
    vis_w = (pos_w >= 0) & (dist >= 0) & (dist <= WINDOW)
    kw = wk_s[pl.ds(pl.multiple_of(i * R, R), WINDOW + R), :]
    vw = jnp.concatenate([wv_s[i + t] for t in range(w_tiles)], axis=1)

    p_c = _softmax_keys_on_rows(_dot(kc_s[...], q), vis_c, 8)
    o_c = _dot(vc_s[...], p_c.astype(BF16))
    keys = []
    for g in range(B_KV):
        p_sum = p_c[:, 4 * g * R:(4 * g + 1) * R]
        for h in range(4 * g + 1, 4 * g + 4):
            p_sum = p_sum + p_c[:, h * R:(h + 1) * R]
        keys.append(_block_keys_on_rows(_dot_f32_exact_lhs(ovT_ref[...], p_sum), qpos, n_slc))
    sel = _topk_mask_keys_on_rows(jnp.concatenate(keys, axis=1), float(min(SLC_TOPN, n_slc))).astype(BF16)

    def slc_bias(c):
        n_blk = sel.shape[0]
        key_blk = lax.shift_right_logical(c * P_CH + lax.broadcasted_iota(jnp.int32, (P_CH, n_blk), 0), 6)
        member = jnp.where(key_blk == lax.broadcasted_iota(jnp.int32, (P_CH, n_blk), 1), 1.0, 0.0).astype(BF16)
        kpos = c * P_CH + lax.broadcasted_iota(jnp.int32, (P_CH, R), 0)
        chosen = _dot(member, sel) > 0.5
        return [jnp.where(chosen[:, g * R:(g + 1) * R] & (kpos <= qpos), 0.0, NEG) for g in range(B_KV)]

    o_s = _attn_keys_on_rows(qr, lambda c: key_rows(sk_s, c), lambda c: sv_s[c], slc_bias, nch, 8,
                             _per_head_row(kn_s[0:1, :], R))
    p_w = _softmax_keys_on_rows(_dot(kw, qr), vis_w, 8)
    o_w = _dot(vw, p_w.astype(BF16))

    def mix(h, o_s_h):
        cols = slice(h * R, (h + 1) * R)
        return (gates[3 * h:3 * h + 1, :] * o_c[:, cols] + gates[3 * h + 1:3 * h + 2, :] * o_s_h
                + gates[3 * h + 2:3 * h + 3, :] * o_w[:, cols])

    o_ref[...] = _heads_to_tokens(_per_head(o_s, 8, mix)).astype(o_ref.dtype)


def _nsa_prompt(qbT, qbrT, gbT, kc, vc, ovT, sk, svT, wk, wvT, *, n, s):
    nq = s // Q_BLOCK
    n_cmp_rows = s // CMP_STRIDE
    n_cmp = (s - CMP_LEN) // CMP_STRIDE + 1
    qrow = lambda b, i: (b * nq + i, 0)
    qcol = lambda b, i: (b, 0, i)
    per_b = lambda b, i: (b, 0)
    full = lambda b, i: (b, 0, 0)
    fixed = lambda b, i: (0, 0)
    return pl.pallas_call(
        functools.partial(_nsa_prompt_kernel, s_len=s, n_cmp=n_cmp),
        grid=(n, nq),
        in_specs=[pl.BlockSpec((1, 512, Q_BLOCK), qcol), pl.BlockSpec((1, 512, Q_BLOCK), qcol),
                  pl.BlockSpec((1, 3 * B_HEADS, Q_BLOCK), qcol),
                  pl.BlockSpec((n_cmp_rows, LANES), per_b), pl.BlockSpec((n_cmp_rows, LANES), per_b),
                  pl.BlockSpec(ovT.shape, fixed),
                  pl.BlockSpec((s, LANES), per_b), pl.BlockSpec((1, LANES, s), full),
                  pl.BlockSpec((s, LANES), per_b), pl.BlockSpec((1, LANES, s), full)],
        out_specs=pl.BlockSpec((Q_BLOCK, 512), qrow),
        out_shape=jax.ShapeDtypeStruct((n * s, 512), BF16),
        scratch_shapes=[pltpu.VMEM((n_cmp_rows, LANES), BF16), pltpu.VMEM((LANES, n_cmp_rows), BF16),
                        pltpu.VMEM((s, LANES), BF16), pltpu.VMEM((s // P_CH, LANES, P_CH), BF16),
                        pltpu.VMEM((WINDOW + s, LANES), BF16),
                        pltpu.VMEM(((WINDOW + s) // Q_BLOCK, LANES, Q_BLOCK), BF16),
                        pltpu.VMEM((8, LANES), F32)],
        compiler_params=_cparams(("arbitrary", "arbitrary")),
        name="nsa_prompt",
    )(qbT, qbrT, gbT, kc, vc, ovT, sk, svT, wk, wvT)


S_CH = 640
S_ROWS = 16


def _page_dst(buf, p, page):
    c, off = divmod(p * page, S_CH)
    return c, off


def _start_all(copies):
    for cp in copies:
        cp.start()
    return copies


def _page_copies_chunked(pt_ref, b, layer, pool, buf, sem, n_pages, page, grouped):
    copies = []
    for p in range(n_pages):
        pg = pt_ref[b, p]
        c, off = _page_dst(buf, p, page)
        if grouped:
            for g in range(2):
                copies.append(pltpu.make_async_copy(
                    pool.at[layer, pg, g], buf.at[c, pl.ds(g * HEAD_DIM, HEAD_DIM), pl.ds(off, page)], sem))
        else:
            copies.append(pltpu.make_async_copy(pool.at[layer, pg], buf.at[c, :, pl.ds(off, page)], sem))
    return copies


def _page_copies_flat(pt_ref, b, layer, pool, buf, sem, n_pages, page):
    copies = []
    for p in range(n_pages):
        pg = pt_ref[b, p]
        for g in range(2):
            copies.append(pltpu.make_async_copy(
                pool.at[layer, pg, g], buf.at[pl.ds(g * HEAD_DIM, HEAD_DIM), pl.ds(p * page, page)], sem))
    return copies


def _set_new_column(buf, past, col):
    c, off = divmod(past, S_CH)
    ch = buf.shape[1]
    lane = lax.broadcasted_iota(jnp.int32, (ch, LANES), 1)
    buf[c, :, off:off + LANES] = jnp.where(lane == 0, jnp.broadcast_to(col, (ch, LANES)), 0.0)


def _pad_heads(q8):
    row = lax.broadcasted_iota(jnp.int32, q8.shape, 0)
    lane = lax.broadcasted_iota(jnp.int32, q8.shape, 1)
    q8 = jnp.where((row // 4) == (lane // HEAD_DIM), q8, 0.0)
    return jnp.concatenate([q8, jnp.zeros_like(q8)], axis=0).astype(BF16)


def _dsa_sample_kernel(pt_ref, iq_ref, iw_ref, qa_ref, ikn_ref, kan_ref, van_ref, cik_hbm, ck_hbm, cv_hbm,
                       o_ref, ik_bufs, k_bufs, v_bufs, keys_s, sel_s, sems, *, layer, past, n_pages, page, k_sel):
    b = pl.program_id(0)
    slot = lax.rem(b, 2)

    def copies(seq, into):
        return (_page_copies_chunked(pt_ref, seq, layer, cik_hbm, ik_bufs.at[into], sems.at[into, 0], n_pages, page, False)
                + _page_copies_chunked(pt_ref, seq, layer, ck_hbm, k_bufs.at[into], sems.at[into, 1], n_pages, page, True)
                + _page_copies_chunked(pt_ref, seq, layer, cv_hbm, v_bufs.at[into], sems.at[into, 2], n_pages, page, True))

    @pl.when(b == 0)
    def _():
        _start_all(copies(0, 0))

    @pl.when(b + 1 < pl.num_programs(0))
    def _():
        _start_all(copies(b + 1, 1 - slot))

    ik_b, k_b, v_b = ik_bufs.at[slot], k_bufs.at[slot], v_bufs.at[slot]
    _set_new_column(ik_b, past, ikn_ref[0][0:IDX_DIM, :])
    _set_new_column(k_b, past, kan_ref[0])
    _set_new_column(v_b, past, van_ref[0])
    for cp in copies(b, slot):
        cp.wait()

    nch = ik_b.shape[0]
    iq16 = jnp.concatenate([iq_ref[0], jnp.zeros((8, IDX_DIM), F32)], axis=0).astype(BF16)
    w16 = jnp.concatenate([iw_ref[0] * (IDX_HEADS ** -0.5), jnp.zeros((8, 1), F32)], axis=0)

    def score_body(c, carry):
        d = jnp.maximum(_dot(iq16, ik_b[c].astype(BF16)), 0.0)
        tot = jnp.sum(w16 * d, axis=0, keepdims=True)
        kpos = c * S_CH + lax.broadcasted_iota(jnp.int32, (1, S_CH), 1)
        key = jnp.where(kpos <= past, _sort_key(tot), INT_MIN)
        keys_s[c] = jnp.broadcast_to(key, (8, S_CH))
        return carry

    lax.fori_loop(0, nch, score_body, 0)

    def put_sel(c, v):
        sel_s[c] = v

    _topk_select(lambda c: keys_s[c], put_sel, nch, float(k_sel), 8, S_CH)
    o = _attn_all_chunks(_pad_heads(qa_ref[0]), lambda c: k_b[c].astype(BF16), lambda c: v_b[c].astype(BF16),
                     lambda c: sel_s[c][0:1, :], nch)
    o_ref[0] = o[0:8]


def _dsa_sample(page_table, iq, iw, qa, ikn, kan, van, c_ik, c_k, c_v, *, layer, past):
    n = page_table.shape[0]
    n_pages = page_table.shape[1]
    page = c_k.shape[-1]
    nch = (past + LANES) // S_CH
    per_seq = lambda width, lanes: pl.BlockSpec((1, width, lanes), lambda b, pt: (b, 0, 0))
    anyspec = pl.BlockSpec(memory_space=pl.ANY)
    grid_spec = pltpu.PrefetchScalarGridSpec(
        num_scalar_prefetch=1,
        grid=(n,),
        in_specs=[per_seq(8, IDX_DIM), per_seq(8, 1), per_seq(8, LANES),
                  per_seq(LANES, 1), per_seq(LANES, 1), per_seq(LANES, 1), anyspec, anyspec, anyspec],
        out_specs=per_seq(8, LANES),
        scratch_shapes=[pltpu.VMEM((2, nch, IDX_DIM, S_CH), F32), pltpu.VMEM((2, nch, LANES, S_CH), F32),
                        pltpu.VMEM((2, nch, LANES, S_CH), F32),
                        pltpu.VMEM((nch, 8, S_CH), jnp.int32), pltpu.VMEM((nch, 8, S_CH), F32),
                        pltpu.SemaphoreType.DMA((2, 3))],
    )
    return pl.pallas_call(
        functools.partial(_dsa_sample_kernel, layer=layer, past=past, n_pages=n_pages, page=page,
                          k_sel=min(DSA_TOPK, (past + 1) // 4)),
        grid_spec=grid_spec,
        out_shape=jax.ShapeDtypeStruct((n, 8, LANES), F32),
        compiler_params=_cparams(("arbitrary",)),
        name="dsa_sample",
    )(page_table, iq, iw, qa, ikn, kan, van, c_ik, c_k, c_v)


def _nsa_sample_kernel(pt_ref, qb_ref, qbr_ref, g_ref, skn_ref, svn_ref, wkn_ref, wvn_ref, wcmpk_ref, wcmpv_ref,
                       ov_ref, swk_ref, swv_ref, cck_hbm, ccv_hbm, csk_hbm, csv_hbm,
                       o_ref, nwk_ref, nwv_ref, ckT_bufs, cvT_bufs, ck_b, cv_b, sk_bufs, sv_bufs, sems,
                       *, layer, past, n_pages, page, n_cmp):
    b = pl.program_id(0)
    slot = lax.rem(b, 2)

    def copies(seq, into):
        return (_page_copies_flat(pt_ref, seq, layer, cck_hbm, ckT_bufs.at[into], sems.at[into, 0], n_pages, page)
                + _page_copies_flat(pt_ref, seq, layer, ccv_hbm, cvT_bufs.at[into], sems.at[into, 1], n_pages, page)
                + _page_copies_chunked(pt_ref, seq, layer, csk_hbm, sk_bufs.at[into], sems.at[into, 2], n_pages, page, True)
                + _page_copies_chunked(pt_ref, seq, layer, csv_hbm, sv_bufs.at[into], sems.at[into, 3], n_pages, page, True))

    @pl.when(b == 0)
    def _():
        _start_all(copies(0, 0))

    @pl.when(b + 1 < pl.num_programs(0))
    def _():
        _start_all(copies(b + 1, 1 - slot))

    ckT_b, cvT_b, sk_b, sv_b = ckT_bufs.at[slot], cvT_bufs.at[slot], sk_bufs.at[slot], sv_bufs.at[slot]
    _set_new_column(sk_b, past, skn_ref[0])
    _set_new_column(sv_b, past, svn_ref[0])
    wb = swk_ref.shape[3]
    lane_w = lax.broadcasted_iota(jnp.int32, (LANES, LANES), 1)
    new_k = jnp.where(lane_w == 0, jnp.broadcast_to(wkn_ref[0], (LANES, LANES)), 0.0)
    new_v = jnp.where(lane_w == 0, jnp.broadcast_to(wvn_ref[0], (LANES, LANES)), 0.0)
    kw = jnp.concatenate([swk_ref[0, 0], new_k], axis=1).astype(BF16)
    vw = jnp.concatenate([swv_ref[0, 0], new_v], axis=1).astype(BF16)
    last = lax.broadcasted_iota(jnp.int32, (LANES, wb), 1) == wb - 1
    nwk_ref[0, 0] = jnp.where(last, jnp.broadcast_to(wkn_ref[0], (LANES, wb)), pltpu.roll(swk_ref[0, 0], wb - 1, 1))
    nwv_ref[0, 0] = jnp.where(last, jnp.broadcast_to(wvn_ref[0], (LANES, wb)), pltpu.roll(swv_ref[0, 0], wb - 1, 1))
    for cp in copies(b, slot):
        cp.wait()

    for t in range(past // LANES):
        rows = slice(t * LANES, (t + 1) * LANES)
        ck_b[rows, :] = ckT_b[:, rows].T
        cv_b[rows, :] = cvT_b[:, rows].T
    n_cmp_rows = past // CMP_STRIDE
    n_slc = -(-(past + 1) // SLC_BLK)
    kc = _compress_rows(ck_b, wcmpk_ref, n_cmp_rows).astype(BF16)
    vc = _compress_rows(cv_b, wcmpv_ref, n_cmp_rows).astype(BF16)

    q16 = _pad_heads(qb_ref[0])
    qr16 = _pad_heads(qbr_ref[0])
    qpos = jnp.full((S_ROWS, 1), past, jnp.int32)
    cidx = lax.broadcasted_iota(jnp.int32, (S_ROWS, n_cmp_rows), 1)
    vis_c = (cidx * CMP_STRIDE + CMP_LEN - 1 <= qpos) & (cidx < n_cmp)
    p_c = _softmax_once(_dot_nt(q16, kc), vis_c)
    o_c = _dot(p_c.astype(BF16), vc)
    row = lax.broadcasted_iota(jnp.int32, p_c.shape, 0)
    g0 = jnp.sum(jnp.where(row < 4, p_c, 0.0), axis=0, keepdims=True)
    g1 = jnp.sum(jnp.where((row >= 4) & (row < 8), p_c, 0.0), axis=0, keepdims=True)
    p_sum = jnp.where(row < 4, jnp.broadcast_to(g0, p_c.shape), jnp.broadcast_to(g1, p_c.shape))
    keys = _block_keys(_dot_f32_exact_rhs(p_sum, ov_ref[...]), qpos, n_slc)
    sel_box = []
    _topk_select(lambda c: keys, lambda c, v: sel_box.append(v), 1, float(min(SLC_TOPN, n_slc)), S_ROWS,
                 keys.shape[1])
    sel_b = sel_box[0]

    def slc_mask(c):
        kpos = c * S_CH + lax.broadcasted_iota(jnp.int32, (S_ROWS, S_CH), 1)
        return jnp.where(kpos <= qpos, _expand_blocks(sel_b, c, S_CH), 0.0)

    o_s = _attn_all_chunks(qr16, lambda c: sk_b[c].astype(BF16), lambda c: sv_b[c].astype(BF16), slc_mask,
                       sk_b.shape[0])
    pos_w = past - wb + lax.broadcasted_iota(jnp.int32, (S_ROWS, wb + LANES), 1)
    dist = qpos - pos_w
    vis_w = (pos_w >= 0) & (dist >= 0) & (dist <= WINDOW)
    p_w = _softmax_once(_dot(qr16, kw), vis_w)
    o_w = _dot_nt(p_w.astype(BF16), vw)
    g = g_ref[0]
    o_ref[0] = g[:, 0:1] * o_c[0:8] + g[:, 1:2] * o_s[0:8] + g[:, 2:3] * o_w[0:8]


def _nsa_sample(page_table, qb, qbr, g, skn, svn, wkn, wvn, w_ck, w_cv, ov, s_wk, s_wv, c_ck, c_cv, c_sk, c_sv,
                *, layer, past):
    n = page_table.shape[0]
    n_pages = page_table.shape[1]
    page = c_ck.shape[-1]
    wb = s_wk.shape[3]
    nch = (past + LANES) // S_CH
    n_cmp = (past + 1 - CMP_LEN) // CMP_STRIDE + 1
    per_seq = lambda width, lanes: pl.BlockSpec((1, width, lanes), lambda b, pt: (b, 0, 0))
    fixed = lambda shape: pl.BlockSpec(shape, lambda b, pt: (0,) * len(shape))
    state = pl.BlockSpec((1, 1, LANES, wb), lambda b, pt: (layer, b, 0, 0))
    new_state = pl.BlockSpec((1, 1, LANES, wb), lambda b, pt: (0, b, 0, 0))
    anyspec = pl.BlockSpec(memory_space=pl.ANY)
    grid_spec = pltpu.PrefetchScalarGridSpec(
        num_scalar_prefetch=1,
        grid=(n,),
        in_specs=[per_seq(8, LANES), per_seq(8, LANES), per_seq(8, 3)] + [per_seq(LANES, 1)] * 4
        + [fixed((CMP_LEN, LANES)), fixed((CMP_LEN, LANES)), fixed(ov.shape), state, state]
        + [anyspec] * 4,
        out_specs=[per_seq(8, LANES), new_state, new_state],
        scratch_shapes=[pltpu.VMEM((2, LANES, past), F32)] * 2 + [pltpu.VMEM((past, LANES), F32)] * 2
        + [pltpu.VMEM((2, nch, LANES, S_CH), F32)] * 2 + [pltpu.SemaphoreType.DMA((2, 4))],
    )
    st_shape = jax.ShapeDtypeStruct((1, n, LANES, wb), F32)
    return pl.pallas_call(
        functools.partial(_nsa_sample_kernel, layer=layer, past=past, n_pages=n_pages, page=page, n_cmp=n_cmp),
        grid_spec=grid_spec,
        out_shape=[jax.ShapeDtypeStruct((n, 8, LANES), F32), st_shape, st_shape],
        compiler_params=_cparams(("arbitrary",)),
        name="nsa_sample",
    )(page_table, qb, qbr, g, skn, svn, wkn, wvn, w_ck, w_cv, ov, s_wk, s_wv, c_ck, c_cv, c_sk, c_sv)


def _merge_kernel(x_ref, oa_ref, ob_ref, gate_ref, wpa_ref, wpb_ref, wo_ref, o_ref):
    gate = gate_ref[...]
    m = (gate[:, :D_MODEL] * _dot(oa_ref[...], wpa_ref[...]) + gate[:, D_MODEL:] * _dot(ob_ref[...], wpb_ref[...]))
    o_ref[...] = x_ref[...] + _dot(m.astype(BF16), wo_ref[...])


def _merge(x, o_a, o_b, gate, w_pa, w_pb, w_o, *, tm):
    T = x.shape[0]
    row = lambda i: (i, 0)
    fixed = lambda i: (0, 0)
    return pl.pallas_call(
        _merge_kernel,
        grid=(T // tm,),
        in_specs=[pl.BlockSpec((tm, D_MODEL), row), pl.BlockSpec((tm, 512), row), pl.BlockSpec((tm, 512), row),
                  pl.BlockSpec((tm, 2 * D_MODEL), row), pl.BlockSpec((512, D_MODEL), fixed),
                  pl.BlockSpec((512, D_MODEL), fixed), pl.BlockSpec((D_MODEL, D_MODEL), fixed)],
        out_specs=pl.BlockSpec((tm, D_MODEL), row),
        out_shape=jax.ShapeDtypeStruct((T, D_MODEL), F32),
        compiler_params=_cparams(("arbitrary",)),
        name="merge",
    )(x, o_a, o_b, gate, w_pa, w_pb, w_o)


def _align_w_in(w):
    z = lambda n: jnp.zeros((w.shape[0], n), w.dtype)
    return jnp.concatenate([w[:, :1352], z(56), w[:, 1352:2632], w[:, 2632:2656], z(104), w[:, 2656:]], axis=1)


def _heads_from_rows(o):
    n = o.shape[0]
    return jnp.concatenate([o[:, :4, :HEAD_DIM].reshape(n, 256), o[:, 4:, HEAD_DIM:].reshape(n, 256)], axis=1)


def _positions_last(x):
    lead = x.shape[:-3]
    nd = len(lead)
    x = jnp.transpose(x, tuple(range(nd)) + (nd + 1, nd + 2, nd))
    return x.reshape(lead + (x.shape[-3] * x.shape[-2], x.shape[-1]))


def _positions_first(x, kv):
    lead = x.shape[:-2]
    nd = len(lead)
    x = x.reshape(lead + (kv, x.shape[-2] // kv, x.shape[-1]))
    return jnp.transpose(x, tuple(range(nd)) + (nd + 2, nd, nd + 1))


def kernel(x_prompt, x_sample, cache_dsa_k, cache_dsa_v, cache_dsa_idx_k, cache_nsa_cmp_k, cache_nsa_cmp_v, cache_nsa_slc_k, cache_nsa_slc_v, state_nsa_win_k, state_nsa_win_v, page_table, norm_ffn1, w_ffn1_up, w_ffn1_down, norm_mix, w_in, w_cmp_k, w_cmp_v, w_proj_a, w_proj_b, w_out, norm_ffn2, w_ffn2_up, w_ffn2_down, norm_final):
    n_p, s_len, _ = x_prompt.shape
    n_s, t_s, _ = x_sample.shape
    assert t_s == 1 and s_len % P_CH == 0
    depth, n_pool, page = cache_dsa_k.shape[:3]
    past = page_table.shape[1] * page
    wb = state_nsa_win_k.shape[2]
    assert (past + LANES) % S_CH == 0 and wb == WINDOW and page == LANES

    xp = x_prompt.reshape(n_p * s_len, D_MODEL)
    xs = x_sample.reshape(n_s, D_MODEL)
    tab_p = _rope_tables(jnp.arange(s_len))
    tab_s = _rope_tables(jnp.full((n_s,), past))
    grouped = lambda c: jnp.transpose(c, (0, 1, 3, 4, 2))
    c_k, c_v = grouped(cache_dsa_k), grouped(cache_dsa_v)
    c_ck, c_cv, c_sk, c_sv = (grouped(cache_nsa_cmp_k), grouped(cache_nsa_cmp_v), grouped(cache_nsa_slc_k),
                              grouped(cache_nsa_slc_v))
    c_ik = jnp.transpose(cache_dsa_idx_k, (0, 1, 3, 2))
    s_wk = _positions_last(state_nsa_win_k)
    s_wv = _positions_last(state_nsa_win_v)
    ov_p = _slc_overlap((s_len - CMP_LEN) // CMP_STRIDE + 1, s_len, s_len // CMP_STRIDE, LANES)
    ov_s = _slc_overlap((past + 1 - CMP_LEN) // CMP_STRIDE + 1, past + 1, past // CMP_STRIDE, 2 * LANES)
    row = lambda v: v.reshape(1, -1)
    bf = lambda w: w.astype(BF16)

    st_p, st_s = [], []
    for l in range(depth):
        w_ck = w_cmp_k[l].reshape(CMP_LEN, LANES)
        w_cv = w_cmp_v[l].reshape(CMP_LEN, LANES)
        w_al = bf(_align_w_in(w_in[l]))
        wu1, wd1, wu2, wd2 = bf(w_ffn1_up[l]), bf(w_ffn1_down[l]), bf(w_ffn2_up[l]), bf(w_ffn2_down[l])
        w_pa, w_pb, w_o = bf(w_proj_a[l]), bf(w_proj_b[l]), bf(w_out[l])
        g_fin = row(norm_final)
        last = l == depth - 1

        xp = _ffn(xp, row(norm_ffn1[l]), wu1, wd1, g_fin, tm=512, tf=D_FF // 2, final=False)
        pp = _proj(xp, row(norm_mix[l]), w_al, tab_p, _PROJ_PROMPT_OUTS, tm=256, seq=s_len)
        o_a = _dsa_prompt(pp["iqT"], pp["qaT"], pp["iwT"], pp["ikw"], pp["ka"], pp["vaT"], n=n_p, s=s_len)
        kc, vc = _compress_prompt(pp["ck"], pp["cv"], w_ck, w_cv, n=n_p, s=s_len)
        o_b = _nsa_prompt(pp["qbT"], pp["qbrT"], pp["gbT"], kc, vc, ov_p.T, pp["sk"], pp["svT"], pp["wk"],
                          pp["wvT"], n=n_p, s=s_len)
        xp = _merge(xp, o_a, o_b, pp["gate"], w_pa, w_pb, w_o, tm=512)
        xp = _ffn(xp, row(norm_ffn2[l]), wu2, wd2, g_fin, tm=512, tf=D_FF // 2, final=last)
        st_p.append((pp["kaT"], pp["vaT"], pp["ikT"], pp["ckT"], pp["cvT"], pp["skT"], pp["svT"],
                     pp["wkT"][:, :, s_len - wb:], pp["wvT"][:, :, s_len - wb:]))

        xs = _ffn(xs, row(norm_ffn1[l]), wu1, wd1, g_fin, tm=n_s, tf=D_FF // 2, final=False)
        ps = _proj(xs, row(norm_mix[l]), w_al, tab_s, _PROJ_SAMPLE_OUTS, tm=n_s, seq=n_s)
        heads = lambda q: q.astype(F32).reshape(n_s, 8, HEAD_DIM)
        dup = lambda q: jnp.tile(heads(q), (1, 1, 2))
        col = lambda t: t.reshape(n_s, LANES, 1)
        o_a = _dsa_sample(page_table, heads(ps["iq"]), ps["ikw"][:, IDX_DIM:IDX_DIM + IDX_HEADS].reshape(n_s, 8, 1),
                          dup(ps["qa"]), col(ps["ikw"]), col(ps["ka"]), col(ps["va"]), c_ik, c_k, c_v,
                          layer=l, past=past)
        o_b, nwk, nwv = _nsa_sample(page_table, dup(ps["qb"]), dup(ps["qbr"]),
                                    ps["gb"][:, :3 * B_HEADS].reshape(n_s, 8, 3),
                                    col(ps["sk"]), col(ps["sv"]), col(ps["wk"]), col(ps["wv"]), w_ck, w_cv, ov_s,
                                    s_wk, s_wv, c_ck, c_cv, c_sk, c_sv, layer=l, past=past)
        xs = _merge(xs, bf(_heads_from_rows(o_a)), bf(_heads_from_rows(o_b)), ps["gate"], w_pa, w_pb, w_o, tm=n_s)
        xs = _ffn(xs, row(norm_ffn2[l]), wu2, wd2, g_fin, tm=n_s, tf=D_FF // 2, final=last)
        kv1 = lambda t: t.reshape(n_s, 1, 2, HEAD_DIM)
        st_s.append((kv1(ps["ka"]), kv1(ps["va"]), ps["ikw"][:, :IDX_DIM].reshape(n_s, 1, IDX_DIM), kv1(ps["ck"]),
                     kv1(ps["cv"]), kv1(ps["sk"]), kv1(ps["sv"]), nwk[0], nwv[0]))

    outs = [xp.reshape(n_p, s_len, D_MODEL), xs.reshape(n_s, 1, D_MODEL)]
    for i in range(9):
        sp = jnp.stack([s[i] for s in st_p])
        ss = jnp.stack([s[i] for s in st_s])
        if i == 2:
            sp = jnp.transpose(sp, (0, 1, 3, 2))
        else:
            sp = _positions_first(sp, 2)
        if i >= 7:
            ss = _positions_first(ss, 2)
        outs += [sp, ss]
    return tuple(outs)
```
